```python
import jax, jax.numpy as jnp
from jax import lax
import numpy as np

D_MODEL = 1024
BATCH = 8
SEQ = 4096
DEPTH = 4

N_MEM = 256
D_FF = 2816
D_CONV = D_MODEL // 2
CONV_WIDTH = 31
D_SGU = D_MODEL - D_CONV
SGU_HEADS = 4
SGU_HEAD_DIM = D_SGU // SGU_HEADS
CHUNK = 128
POOL_WINDOWS = (2, 4, 8, 16)
POOL_GROUPS = len(POOL_WINDOWS)
POOL_GROUP_DIM = D_MODEL // POOL_GROUPS
XA_HEADS = 4
XA_HEAD_DIM = D_MODEL // XA_HEADS
N_EVEN = (DEPTH + 1) // 2
N_ODD = DEPTH // 2
EPS = 1e-6

kernel_name = "hybrid_conv_sgu_pool_macaron_xattn"


def rms_norm(x, g):
    xf = x.astype(jnp.float32)
    y = xf * lax.rsqrt(jnp.mean(xf * xf, axis=-1, keepdims=True) + EPS)
    return (y * g.astype(jnp.float32)).astype(x.dtype)


def layer_norm(x, g, b):
    xf = x.astype(jnp.float32)
    mu = jnp.mean(xf, axis=-1, keepdims=True)
    var = jnp.mean(jnp.square(xf - mu), axis=-1, keepdims=True)
    y = (xf - mu) * lax.rsqrt(var + EPS)
    return (y * g.astype(jnp.float32) + b.astype(jnp.float32)).astype(x.dtype)


def swiglu_ffn(h, w_gu, w_down):
    gu = h @ w_gu
    gate, up = jnp.split(gu, 2, axis=-1)
    return (jax.nn.silu(gate) * up) @ w_down


def conformer_conv(a_val, a_gate, conv_w, conv_b, ln_g, ln_b):
    z = a_val * jax.nn.sigmoid(a_gate)
    rhs = conv_w[:, None, :].astype(z.dtype)
    z = lax.conv_general_dilated(
        z, rhs, window_strides=(1,), padding=[(CONV_WIDTH - 1, 0)],
        dimension_numbers=("NWC", "WIO", "NWC"), feature_group_count=z.shape[-1])
    z = z + conv_b
    z = layer_norm(z, ln_g, ln_b)
    return jax.nn.silu(z)


def chunked_causal_sgu(u, v, ln_g, ln_b, w_s, b_s):
    bsz, seq, _ = u.shape
    n_chunk = seq // CHUNK
    v = layer_norm(v, ln_g, ln_b)
    v = v.reshape(bsz, n_chunk, CHUNK, SGU_HEADS, SGU_HEAD_DIM)
    mask = jnp.tril(jnp.ones((CHUNK, CHUNK), dtype=w_s.dtype))
    w = w_s * mask[None]
    mixed = jnp.einsum("hts,bnshc->bnthc", w, v) + b_s.T[:, :, None]
    out = u.reshape(bsz, n_chunk, CHUNK, SGU_HEADS, SGU_HEAD_DIM) * mixed
    return out.reshape(bsz, seq, D_SGU)


def even_mixer(h, w_in, conv_w, conv_b, conv_ln_g, conv_ln_b,
               sgu_ln_g, sgu_ln_b, sgu_w, sgu_b, w_out):
    p = h @ w_in
    a_val = p[..., :D_CONV]
    a_gate = p[..., D_CONV:2 * D_CONV]
    zb = jax.nn.gelu(p[..., 2 * D_CONV:], approximate=False)
    b_u = zb[..., :D_SGU]
    b_v = zb[..., D_SGU:]
    ya = conformer_conv(a_val, a_gate, conv_w, conv_b, conv_ln_g, conv_ln_b)
    yb = chunked_causal_sgu(b_u, b_v, sgu_ln_g, sgu_ln_b, sgu_w, sgu_b)
    return jnp.concatenate([ya, yb], axis=-1) @ w_out


def odd_mixer(h, w_in, w_group, scale, w_out):
    p = h @ w_in
    bsz, seq, _ = p.shape
    pf = p.astype(jnp.float32)
    csum = jnp.cumsum(pf, axis=1)
    c_pad = jnp.concatenate([jnp.zeros((bsz, 1, D_MODEL), jnp.float32), csum], axis=1)
    pos = jnp.arange(1, seq + 1, dtype=jnp.int32)
    pooled = []
    for g, w in enumerate(POOL_WINDOWS):
        sl = slice(g * POOL_GROUP_DIM, (g + 1) * POOL_GROUP_DIM)
        upper = c_pad[:, 1:, sl]
        lower = jnp.concatenate(
            [jnp.zeros((bsz, w - 1, POOL_GROUP_DIM), jnp.float32), c_pad[:, :seq + 1 - w, sl]], axis=1)
        count = jnp.minimum(pos, w).astype(jnp.float32)[None, :, None]
        pooled.append((upper - lower) / count)
    d = (jnp.concatenate(pooled, axis=-1) - pf).astype(p.dtype)
    d = d.reshape(bsz, seq, POOL_GROUPS, POOL_GROUP_DIM)
    d = jnp.einsum("bsgc,gcd->bsgd", d, w_group).reshape(bsz, seq, D_MODEL)
    return (d * scale) @ w_out


def memory_cross_attention(h, mem, mem_g, w_q, w_kv, w_o):
    bsz, seq, _ = h.shape
    q = (h @ w_q).reshape(bsz, seq, XA_HEADS, XA_HEAD_DIM)
    kv = rms_norm(mem, mem_g) @ w_kv
    k, v = jnp.split(kv, 2, axis=-1)
    k = k.reshape(bsz, -1, XA_HEADS, XA_HEAD_DIM)
    v = v.reshape(bsz, -1, XA_HEADS, XA_HEAD_DIM)
    s = jnp.einsum("bshd,bmhd->bhsm", q, k).astype(jnp.float32) * (XA_HEAD_DIM ** -0.5)
    a = jax.nn.softmax(s, axis=-1).astype(v.dtype)
    o = jnp.einsum("bhsm,bmhd->bshd", a, v).reshape(bsz, seq, D_MODEL)
    return o @ w_o


def setup_inputs(seed: int = 0) -> dict:
    key = jax.random.key(seed)
    ks = jax.random.split(key, 32)
    f32 = jnp.float32

    def nrm(k, shape, fan_in):
        return jax.random.normal(k, shape, f32) * (fan_in ** -0.5)

    def gain(k, shape):
        return 1.0 + 0.05 * jax.random.normal(k, shape, f32)

    def small(k, shape):
        return 0.02 * jax.random.normal(k, shape, f32)

    return {
        "x": jax.random.normal(ks[0], (BATCH, SEQ, D_MODEL), f32),
        "mem": jax.random.normal(ks[1], (BATCH, N_MEM, D_MODEL), f32),
        "ffn1_pre_g": gain(ks[2], (DEPTH, D_MODEL)),
        "ffn1_w_gu": nrm(ks[3], (DEPTH, D_MODEL, 2 * D_FF), D_MODEL),
        "ffn1_w_down": nrm(ks[4], (DEPTH, D_FF, D_MODEL), D_FF),
        "ffn1_post_g": gain(ks[5], (DEPTH, D_MODEL)),
        "mix_pre_g": gain(ks[6], (DEPTH, D_MODEL)),
        "mix_post_g": gain(ks[7], (DEPTH, D_MODEL)),
        "ev_w_in": nrm(ks[8], (N_EVEN, D_MODEL, 2 * D_CONV + 2 * D_SGU), D_MODEL),
        "ev_conv_w": nrm(ks[9], (N_EVEN, CONV_WIDTH, D_CONV), CONV_WIDTH),
        "ev_conv_b": small(ks[10], (N_EVEN, D_CONV)),
        "ev_conv_ln_g": gain(ks[11], (N_EVEN, D_CONV)),
        "ev_conv_ln_b": small(ks[12], (N_EVEN, D_CONV)),
        "ev_sgu_ln_g": gain(ks[13], (N_EVEN, D_SGU)),
        "ev_sgu_ln_b": small(ks[14], (N_EVEN, D_SGU)),
        "ev_sgu_w": nrm(ks[15], (N_EVEN, SGU_HEADS, CHUNK, CHUNK), CHUNK),
        "ev_sgu_b": gain(ks[16], (N_EVEN, SGU_HEADS, CHUNK)),
        "ev_w_out": nrm(ks[17], (N_EVEN, D_CONV + D_SGU, D_MODEL), D_CONV + D_SGU),
        "od_w_in": nrm(ks[18], (N_ODD, D_MODEL, D_MODEL), D_MODEL),
        "od_w_group": nrm(ks[19], (N_ODD, POOL_GROUPS, POOL_GROUP_DIM, POOL_GROUP_DIM), POOL_GROUP_DIM),
        "od_scale": gain(ks[20], (N_ODD, D_MODEL)),
        "od_w_out": nrm(ks[21], (N_ODD, D_MODEL, D_MODEL), D_MODEL),
        "xa_pre_g": gain(ks[22], (DEPTH, D_MODEL)),
        "xa_mem_g": gain(ks[23], (DEPTH, D_MODEL)),
        "xa_w_q": nrm(ks[24], (DEPTH, D_MODEL, D_MODEL), D_MODEL),
        "xa_w_kv": nrm(ks[25], (DEPTH, D_MODEL, 2 * D_MODEL), D_MODEL),
        "xa_w_o": nrm(ks[26], (DEPTH, D_MODEL, D_MODEL), D_MODEL),
        "xa_post_g": gain(ks[27], (DEPTH, D_MODEL)),
        "ffn2_pre_g": gain(ks[28], (DEPTH, D_MODEL)),
        "ffn2_w_gu": nrm(ks[29], (DEPTH, D_MODEL, 2 * D_FF), D_MODEL),
        "ffn2_w_down": nrm(ks[30], (DEPTH, D_FF, D_MODEL), D_FF),
        "ffn2_post_g": gain(ks[31], (DEPTH, D_MODEL)),
    }


def reference(x, mem, ffn1_pre_g, ffn1_w_gu, ffn1_w_down, ffn1_post_g,
              mix_pre_g, mix_post_g,
              ev_w_in, ev_conv_w, ev_conv_b, ev_conv_ln_g, ev_conv_ln_b,
              ev_sgu_ln_g, ev_sgu_ln_b, ev_sgu_w, ev_sgu_b, ev_w_out,
              od_w_in, od_w_group, od_scale, od_w_out,
              xa_pre_g, xa_mem_g, xa_w_q, xa_w_kv, xa_w_o, xa_post_g,
              ffn2_pre_g, ffn2_w_gu, ffn2_w_down, ffn2_post_g):
    h = x
    for i in range(DEPTH):
        f = swiglu_ffn(rms_norm(h, ffn1_pre_g[i]), ffn1_w_gu[i], ffn1_w_down[i])
        h = h + 0.5 * rms_norm(f, ffn1_post_g[i])
        hn = rms_norm(h, mix_pre_g[i])
        if i % 2 == 0:
            e = i // 2
            m = even_mixer(hn, ev_w_in[e], ev_conv_w[e], ev_conv_b[e],
                           ev_conv_ln_g[e], ev_conv_ln_b[e], ev_sgu_ln_g[e],
                           ev_sgu_ln_b[e], ev_sgu_w[e], ev_sgu_b[e], ev_w_out[e])
        else:
            o = i // 2
            m = odd_mixer(hn, od_w_in[o], od_w_group[o], od_scale[o], od_w_out[o])
        h = h + rms_norm(m, mix_post_g[i])
        c = memory_cross_attention(rms_norm(h, xa_pre_g[i]), mem, xa_mem_g[i],
                                   xa_w_q[i], xa_w_kv[i], xa_w_o[i])
        h = h + rms_norm(c, xa_post_g[i])
        f = swiglu_ffn(rms_norm(h, ffn2_pre_g[i]), ffn2_w_gu[i], ffn2_w_down[i])
        h = h + 0.5 * rms_norm(f, ffn2_post_g[i])
    return h
```

```python
import functools

import jax
import jax.numpy as jnp
from jax import lax
from jax.experimental import pallas as pl
from jax.experimental.pallas import tpu as pltpu

F32 = jnp.float32
BF16 = jnp.bfloat16

EPS = 1e-6
SGU_HEADS = 4
SGU_CHUNK = 128
POOL_WINDOWS = (2, 4, 8, 16)
XA_HEADS = 4

ROW_TILE = 512
FF_CHUNK = 1024
CONV_HALO = 32
POOL_HALO = 16
VMEM_LIMIT = 56 * 1024 * 1024


def _rms(x, g):
    return x * lax.rsqrt(jnp.mean(x * x, axis=-1, keepdims=True) + EPS) * g


def _layer_norm(x, g, b):
    mu = jnp.mean(x, axis=-1, keepdims=True)
    xc = x - mu
    var = jnp.mean(xc * xc, axis=-1, keepdims=True)
    return xc * lax.rsqrt(var + EPS) * g + b


def _dot(a, b):
    return jnp.dot(a, b, preferred_element_type=F32)


def _resident(shape):
    zeros = (0,) * len(shape)
    return pl.BlockSpec(shape, lambda i: zeros, pipeline_mode=pl.Buffered(1))


def _rows(tm, d):
    return pl.BlockSpec((tm, d), lambda i: (i, 0))


def _params():
    return pltpu.CompilerParams(dimension_semantics=("arbitrary",), vmem_limit_bytes=VMEM_LIMIT)


def _ffn_body(x_ref, pre_g_ref, wg_ref, wu_ref, wd_ref, post_g_ref, o_ref, *, chunks):
    x = x_ref[...]
    hn = _rms(x, pre_g_ref[...]).astype(BF16)
    f = None
    for c0, c1 in chunks:
        g = _dot(hn, wg_ref[:, c0:c1])
        u = _dot(hn, wu_ref[:, c0:c1])
        a = (g * jax.nn.sigmoid(g) * u).astype(BF16)
        fc = _dot(a, wd_ref[c0:c1, :])
        f = fc if f is None else f + fc
    o_ref[...] = x + 0.5 * _rms(f, post_g_ref[...])


def _ffn(h, pre_g, wg, wu, wd, post_g):
    n, d = h.shape
    d_ff = wg.shape[1]
    chunks = tuple((c, min(c + FF_CHUNK, d_ff)) for c in range(0, d_ff, FF_CHUNK))
    return pl.pallas_call(
        functools.partial(_ffn_body, chunks=chunks),
        out_shape=jax.ShapeDtypeStruct((n, d), F32),
        grid=(n // ROW_TILE,),
        in_specs=[_rows(ROW_TILE, d), _resident((1, d)), _resident((d, d_ff)), _resident((d, d_ff)),
                  _resident((d_ff, d)), _resident((1, d))],
        out_specs=_rows(ROW_TILE, d),
        compiler_params=_params(),
        name="swiglu_ffn",
    )(h, pre_g, wg, wu, wd, post_g)


def _kv_body(mem_ref, g_ref, wkt_ref, wv_ref, kt_ref, v_ref):
    memn = _rms(mem_ref[0], g_ref[...]).astype(BF16)
    kt = lax.dot_general(wkt_ref[...], memn, (((1,), (1,)), ((), ())), preferred_element_type=F32)
    kt_ref[0] = kt.astype(BF16)
    v_ref[0] = _dot(memn, wv_ref[...]).astype(BF16)


def _kv(mem, g, wkt, wv):
    b, m, d = mem.shape
    return pl.pallas_call(
        _kv_body,
        out_shape=(jax.ShapeDtypeStruct((b, d, m), BF16), jax.ShapeDtypeStruct((b, m, d), BF16)),
        grid=(b,),
        in_specs=[pl.BlockSpec((1, m, d), lambda i: (i, 0, 0)), _resident((1, d)), _resident((d, d)),
                  _resident((d, d))],
        out_specs=(pl.BlockSpec((1, d, m), lambda i: (i, 0, 0)), pl.BlockSpec((1, m, d), lambda i: (i, 0, 0))),
        compiler_params=_params(),
        name="xattn_kv",
    )(mem, g, wkt, wv)


def _xa_body(x_ref, pre_g_ref, wq_ref, kt_ref, v_ref, wo_ref, post_g_ref, o_ref, *, heads):
    x = x_ref[...]
    d = x.shape[1]
    hd = d // heads
    hn = _rms(x, pre_g_ref[...]).astype(BF16)
    q = (_dot(hn, wq_ref[...]) * (hd ** -0.5)).astype(BF16)
    outs = []
    for h in range(heads):
        sl = slice(h * hd, (h + 1) * hd)
        s = _dot(q[:, sl], kt_ref[0, sl, :])
        p = jnp.exp(s - jnp.max(s, axis=-1, keepdims=True))
        l = jnp.sum(p, axis=-1, keepdims=True)
        outs.append((_dot(p.astype(BF16), v_ref[0, :, sl]) / l).astype(BF16))
    c = _dot(jnp.concatenate(outs, axis=-1), wo_ref[...])
    o_ref[...] = x + _rms(c, post_g_ref[...])


def _xa(h, pre_g, wq, kt, v, wo, post_g, seq):
    n, d = h.shape
    m = v.shape[1]
    nb = seq // ROW_TILE
    return pl.pallas_call(
        functools.partial(_xa_body, heads=XA_HEADS),
        out_shape=jax.ShapeDtypeStruct((n, d), F32),
        grid=(n // ROW_TILE,),
        in_specs=[_rows(ROW_TILE, d), _resident((1, d)), _resident((d, d)),
                  pl.BlockSpec((1, d, m), lambda i: (i // nb, 0, 0)),
                  pl.BlockSpec((1, m, d), lambda i: (i // nb, 0, 0)),
                  _resident((d, d)), _resident((1, d))],
        out_specs=_rows(ROW_TILE, d),
        compiler_params=_params(),
        name="xattn",
    )(h, pre_g, wq, kt, v, wo, post_g)


def _odd_body(x_ref, pre_g_ref, win_ref, wgrp_ref, scale_ref, wout_ref, post_g_ref, o_ref, pbuf_ref, *, nb):
    tm, d = x_ref.shape
    gd = d // len(POOL_WINDOWS)
    j = pl.program_id(0) % nb

    @pl.when(j == 0)
    def _():
        pbuf_ref[0:POOL_HALO, :] = jnp.zeros((POOL_HALO, d), F32)

    x = x_ref[...]
    hn = _rms(x, pre_g_ref[...]).astype(BF16)
    p = _dot(hn, win_ref[...])
    pbuf_ref[POOL_HALO:, :] = p
    s = pbuf_ref[...]
    pos = j * tm + 1 + lax.broadcasted_iota(jnp.int32, (tm, 1), 0)
    pieces = []
    width = 1
    for gi, w in enumerate(POOL_WINDOWS):
        while width < w:
            s = s + pltpu.roll(s, width, 0)
            width *= 2
        count = jnp.minimum(pos, w).astype(F32)
        sl = slice(gi * gd, (gi + 1) * gd)
        pieces.append((s[POOL_HALO:, sl] / count - p[:, sl]).astype(BF16))
    pbuf_ref[0:POOL_HALO, :] = p[tm - POOL_HALO:, :]
    e = jnp.concatenate([_dot(pieces[gi], wgrp_ref[gi]) for gi in range(len(POOL_WINDOWS))], axis=-1)
    m = _dot((e * scale_ref[...]).astype(BF16), wout_ref[...])
    o_ref[...] = x + _rms(m, post_g_ref[...])


def _odd(h, pre_g, win, wgrp, scale, wout, post_g, seq):
    n, d = h.shape
    g, gd, _ = wgrp.shape
    return pl.pallas_call(
        functools.partial(_odd_body, nb=seq // ROW_TILE),
        out_shape=jax.ShapeDtypeStruct((n, d), F32),
        grid=(n // ROW_TILE,),
        in_specs=[_rows(ROW_TILE, d), _resident((1, d)), _resident((d, d)), _resident((g, gd, gd)),
                  _resident((1, d)), _resident((d, d)), _resident((1, d))],
        out_specs=_rows(ROW_TILE, d),
        scratch_shapes=[pltpu.VMEM((POOL_HALO + ROW_TILE, d), F32)],
        compiler_params=_params(),
        name="pool_mixer",
    )(h, pre_g, win, wgrp, scale, wout, post_g)


def _even_body(x_ref, pre_g_ref, win_ref, convw_ref, convb_ref, clng_ref, clnb_ref, slng_ref, slnb_ref,
               sguw_ref, sgub_ref, wout_ref, post_g_ref, o_ref, zbuf_ref, ycat_ref, *, nb):
    tm, d = x_ref.shape
    dc = convw_ref.shape[1]
    ds = d - dc
    width = convw_ref.shape[0]
    hd = ds // SGU_HEADS
    j = pl.program_id(0) % nb

    @pl.when(j == 0)
    def _():
        zbuf_ref[0:CONV_HALO, :] = jnp.zeros((CONV_HALO, dc), F32)

    x = x_ref[...]
    hn = _rms(x, pre_g_ref[...]).astype(BF16)
    p = _dot(hn, win_ref[...])

    z = p[:, :dc] * jax.nn.sigmoid(p[:, dc:2 * dc])
    zbuf_ref[CONV_HALO:, :] = z
    ext = zbuf_ref[...]
    n_ext = CONV_HALO + tm
    acc = None
    for b in range(8):
        shifted = ext if b == 0 else pltpu.roll(ext, n_ext - b, 0)
        for a in range(CONV_HALO // 8 + 1):
            k = 8 * a + b - (CONV_HALO - (width - 1))
            if 0 <= k < width:
                term = shifted[8 * a:8 * a + tm, :] * convw_ref[k:k + 1, :]
                acc = term if acc is None else acc + term
    zbuf_ref[0:CONV_HALO, :] = z[tm - CONV_HALO:, :]
    ya = _layer_norm(acc + convb_ref[...], clng_ref[...], clnb_ref[...])
    ycat_ref[:, :dc] = (ya * jax.nn.sigmoid(ya)).astype(BF16)

    zb = p[:, 2 * dc:]
    zb = 0.5 * zb * (1.0 + lax.erf(zb * (2.0 ** -0.5)))
    bu = zb[:, :ds]
    bv = _layer_norm(zb[:, ds:], slng_ref[...], slnb_ref[...]).astype(BF16)
    row = lax.broadcasted_iota(jnp.int32, (SGU_CHUNK, SGU_CHUNK), 0)
    col = lax.broadcasted_iota(jnp.int32, (SGU_CHUNK, SGU_CHUNK), 1)
    for hh in range(SGU_HEADS):
        w = jnp.where(col <= row, sguw_ref[hh], 0.0).astype(BF16)
        bias = sgub_ref[hh]
        cs = slice(hh * hd, (hh + 1) * hd)
        for c in range(tm // SGU_CHUNK):
            rs = slice(c * SGU_CHUNK, (c + 1) * SGU_CHUNK)
            mixed = _dot(w, bv[rs, cs]) + bias
            ycat_ref[rs, dc + hh * hd:dc + (hh + 1) * hd] = (bu[rs, cs] * mixed).astype(BF16)

    m = _dot(ycat_ref[...], wout_ref[...])
    o_ref[...] = x + _rms(m, post_g_ref[...])


def _even(h, pre_g, win, convw, convb, clng, clnb, slng, slnb, sguw, sgub, wout, post_g, seq):
    n, d = h.shape
    width, dc = convw.shape
    ds = d - dc
    hd = ds // SGU_HEADS
    assert width - 1 <= CONV_HALO and ROW_TILE % SGU_CHUNK == 0
    return pl.pallas_call(
        functools.partial(_even_body, nb=seq // ROW_TILE),
        out_shape=jax.ShapeDtypeStruct((n, d), F32),
        grid=(n // ROW_TILE,),
        in_specs=[_rows(ROW_TILE, d), _resident((1, d)), _resident((d, 2 * dc + 2 * ds)),
                  _resident((width, dc)), _resident((1, dc)), _resident((1, dc)), _resident((1, dc)),
                  _resident((1, ds)), _resident((1, ds)),
                  _resident((SGU_HEADS, SGU_CHUNK, SGU_CHUNK)), _resident((SGU_HEADS, SGU_CHUNK, hd)),
                  _resident((d, d)), _resident((1, d))],
        out_specs=_rows(ROW_TILE, d),
        scratch_shapes=[pltpu.VMEM((CONV_HALO + ROW_TILE, dc), F32), pltpu.VMEM((ROW_TILE, d), BF16)],
        compiler_params=_params(),
        name="conv_sgu_mixer",
    )(h, pre_g, win, convw, convb, clng, clnb, slng, slnb, sguw, sgub, wout, post_g)


def kernel(x, mem, ffn1_pre_g, ffn1_w_gu, ffn1_w_down, ffn1_post_g, mix_pre_g, mix_post_g, ev_w_in, ev_conv_w, ev_conv_b, ev_conv_ln_g, ev_conv_ln_b, ev_sgu_ln_g, ev_sgu_ln_b, ev_sgu_w, ev_sgu_b, ev_w_out, od_w_in, od_w_group, od_scale, od_w_out, xa_pre_g, xa_mem_g, xa_w_q, xa_w_kv, xa_w_o, xa_post_g, ffn2_pre_g, ffn2_w_gu, ffn2_w_down, ffn2_post_g):
    bsz, seq, d = x.shape
    depth = ffn1_w_gu.shape[0]
    d_ff = ffn1_w_down.shape[1]
    hd = (d - ev_conv_w.shape[2]) // SGU_HEADS
    assert seq % ROW_TILE == 0 and ROW_TILE >= max(CONV_HALO, POOL_HALO)

    def row(v):
        return v.reshape(1, -1)

    h = x.reshape(bsz * seq, d)
    for i in range(depth):
        h = _ffn(h, row(ffn1_pre_g[i]), ffn1_w_gu[i, :, :d_ff].astype(BF16), ffn1_w_gu[i, :, d_ff:].astype(BF16),
                 ffn1_w_down[i].astype(BF16), row(ffn1_post_g[i]))
        if i % 2 == 0:
            e = i // 2
            sgub = jnp.broadcast_to(ev_sgu_b[e][:, :, None], (SGU_HEADS, SGU_CHUNK, hd))
            h = _even(h, row(mix_pre_g[i]), ev_w_in[e].astype(BF16), ev_conv_w[e], row(ev_conv_b[e]),
                      row(ev_conv_ln_g[e]), row(ev_conv_ln_b[e]), row(ev_sgu_ln_g[e]), row(ev_sgu_ln_b[e]),
                      ev_sgu_w[e], sgub, ev_w_out[e].astype(BF16), row(mix_post_g[i]), seq)
        else:
            o = i // 2
            h = _odd(h, row(mix_pre_g[i]), od_w_in[o].astype(BF16), od_w_group[o].astype(BF16), row(od_scale[o]),
                     od_w_out[o].astype(BF16), row(mix_post_g[i]), seq)
        kt, v = _kv(mem, row(xa_mem_g[i]), xa_w_kv[i, :, :d].T.astype(BF16), xa_w_kv[i, :, d:].astype(BF16))
        h = _xa(h, row(xa_pre_g[i]), xa_w_q[i].astype(BF16), kt, v, xa_w_o[i].astype(BF16), row(xa_post_g[i]), seq)
        h = _ffn(h, row(ffn2_pre_g[i]), ffn2_w_gu[i, :, :d_ff].astype(BF16), ffn2_w_gu[i, :, d_ff:].astype(BF16),
                 ffn2_w_down[i].astype(BF16), row(ffn2_post_g[i]))
    return h.reshape(bsz, seq, d)
```

```python
import functools

import jax
import jax.numpy as jnp
from jax import lax
from jax.experimental import pallas as pl
from jax.experimental.pallas import tpu as pltpu

F32 = jnp.float32
BF16 = jnp.bfloat16

EPS = 1e-6
SGU_HEADS = 4
SGU_CHUNK = 128
POOL_WINDOWS = (2, 4, 8, 16)
XA_HEADS = 4

SUBLANES = 8
ROW_TILE = 1024
FFN_SUB_ROWS, FFN_SKEW = 256, 1
XA_SUB_ROWS, XA_SKEW = 512, None
ODD_SUB_ROWS, ODD_SKEW = 256, 1
EVEN_SUB_ROWS, EVEN_SKEW = 512, 2
FF_CHUNK = 1024
CONV_HALO = 32
POOL_HALO = 16
VMEM_LIMIT = 56 * 1024 * 1024


def _rms(x, g):
    return x * lax.rsqrt(jnp.mean(x * x, axis=-1, keepdims=True) + EPS) * g


def _layer_norm(x, g, b):
    mu = jnp.mean(x, axis=-1, keepdims=True)
    xc = x - mu
    var = jnp.mean(xc * xc, axis=-1, keepdims=True)
    return xc * lax.rsqrt(var + EPS) * g + b


def _dot(a, b):
    return jnp.dot(a, b, preferred_element_type=F32)


def _resident(shape):
    zeros = (0,) * len(shape)
    return pl.BlockSpec(shape, lambda i: zeros, pipeline_mode=pl.Buffered(1))


def _rows(tm, d):
    return pl.BlockSpec((tm, d), lambda i: (i, 0))


def _params():
    return pltpu.CompilerParams(dimension_semantics=("arbitrary",), vmem_limit_bytes=VMEM_LIMIT)


def _run_skewed(chains, skew):
    if skew is None:
        for chain in chains:
            for _ in chain:
                pass
        return
    done = [False] * len(chains)
    tick = 0
    while not all(done):
        for c, chain in enumerate(chains):
            if not done[c] and tick >= c * skew:
                try:
                    next(chain)
                except StopIteration:
                    done[c] = True
        tick += 1


def _ffn_body(x_ref, pre_g_ref, wg_ref, wu_ref, wd_ref, post_g_ref, o_ref, *, chunks):
    def chain(rs):
        hn = _rms(x_ref[rs, :], pre_g_ref[...]).astype(BF16)
        yield
        f = None
        for c0, c1 in chunks:
            g = _dot(hn, wg_ref[:, c0:c1])
            u = _dot(hn, wu_ref[:, c0:c1])
            a = (g * jax.nn.sigmoid(g) * u).astype(BF16)
            fc = _dot(a, wd_ref[c0:c1, :])
            f = fc if f is None else f + fc
            yield
        o_ref[rs, :] = x_ref[rs, :] + 0.5 * _rms(f, post_g_ref[...])

    _run_skewed([chain(slice(r0, r0 + FFN_SUB_ROWS)) for r0 in range(0, x_ref.shape[0], FFN_SUB_ROWS)], FFN_SKEW)


def _ffn(h, pre_g, wg, wu, wd, post_g):
    n, d = h.shape
    d_ff = wg.shape[1]
    chunks = tuple((c, min(c + FF_CHUNK, d_ff)) for c in range(0, d_ff, FF_CHUNK))
    return pl.pallas_call(
        functools.partial(_ffn_body, chunks=chunks),
        out_shape=jax.ShapeDtypeStruct((n, d), F32),
        grid=(n // ROW_TILE,),
        in_specs=[_rows(ROW_TILE, d), _resident((1, d)), _resident((d, d_ff)), _resident((d, d_ff)),
                  _resident((d_ff, d)), _resident((1, d))],
        out_specs=_rows(ROW_TILE, d),
        compiler_params=_params(),
        name="swiglu_ffn",
    )(h, pre_g, wg, wu, wd, post_g)


def _kv_body(mem_ref, g_ref, wkt_ref, wv_ref, kt_ref, v_ref):
    memn = _rms(mem_ref[0], g_ref[...]).astype(BF16)
    kt = lax.dot_general(wkt_ref[...], memn, (((1,), (1,)), ((), ())), preferred_element_type=F32)
    kt_ref[0] = kt.astype(BF16)
    v_ref[0] = _dot(memn, wv_ref[...]).astype(BF16)


def _kv(mem, g, wkt, wv):
    b, m, d = mem.shape
    return pl.pallas_call(
        _kv_body,
        out_shape=(jax.ShapeDtypeStruct((b, d, m), BF16), jax.ShapeDtypeStruct((b, m, d), BF16)),
        grid=(b,),
        in_specs=[pl.BlockSpec((1, m, d), lambda i: (i, 0, 0)), _resident((1, d)), _resident((d, d)),
                  _resident((d, d))],
        out_specs=(pl.BlockSpec((1, d, m), lambda i: (i, 0, 0)), pl.BlockSpec((1, m, d), lambda i: (i, 0, 0))),
        compiler_params=_params(),
        name="xattn_kv",
    )(mem, g, wkt, wv)


def _xa_body(x_ref, pre_g_ref, wq_ref, kt_ref, v_ref, wo_ref, post_g_ref, o_ref, *, heads):
    d = x_ref.shape[1]
    hd = d // heads

    def chain(rs):
        hn = _rms(x_ref[rs, :], pre_g_ref[...]).astype(BF16)
        q = (_dot(hn, wq_ref[...]) * (hd ** -0.5)).astype(BF16)
        yield
        outs = []
        for h in range(heads):
            sl = slice(h * hd, (h + 1) * hd)
            s = _dot(q[:, sl], kt_ref[0, sl, :])
            p = jnp.exp(s - jnp.max(s, axis=-1, keepdims=True))
            l = jnp.sum(p, axis=-1, keepdims=True)
            outs.append((_dot(p.astype(BF16), v_ref[0, :, sl]) / l).astype(BF16))
            yield
        c = _dot(jnp.concatenate(outs, axis=-1), wo_ref[...])
        o_ref[rs, :] = x_ref[rs, :] + _rms(c, post_g_ref[...])

    _run_skewed([chain(slice(r0, r0 + XA_SUB_ROWS)) for r0 in range(0, x_ref.shape[0], XA_SUB_ROWS)], XA_SKEW)


def _xa(h, pre_g, wq, kt, v, wo, post_g, seq):
    n, d = h.shape
    m = v.shape[1]
    nb = seq // ROW_TILE
    return pl.pallas_call(
        functools.partial(_xa_body, heads=XA_HEADS),
        out_shape=jax.ShapeDtypeStruct((n, d), F32),
        grid=(n // ROW_TILE,),
        in_specs=[_rows(ROW_TILE, d), _resident((1, d)), _resident((d, d)),
                  pl.BlockSpec((1, d, m), lambda i: (i // nb, 0, 0)),
                  pl.BlockSpec((1, m, d), lambda i: (i // nb, 0, 0)),
                  _resident((d, d)), _resident((1, d))],
        out_specs=_rows(ROW_TILE, d),
        compiler_params=_params(),
        name="xattn",
    )(h, pre_g, wq, kt, v, wo, post_g)


def _odd_body(x_ref, pre_g_ref, win_ref, wgrp_ref, scale_ref, wout_ref, post_g_ref, o_ref, pbuf_ref, *, nb):
    tm, d = x_ref.shape
    gd = d // len(POOL_WINDOWS)
    sub = ODD_SUB_ROWS
    j = pl.program_id(0) % nb

    @pl.when(j == 0)
    def _():
        pbuf_ref[0:POOL_HALO, :] = jnp.zeros((POOL_HALO, d), F32)

    def chain(r0):
        hn = _rms(x_ref[r0:r0 + sub, :], pre_g_ref[...]).astype(BF16)
        p = _dot(hn, win_ref[...])
        pbuf_ref[POOL_HALO + r0:POOL_HALO + r0 + sub, :] = p
        yield
        s = pbuf_ref[r0:POOL_HALO + r0 + sub, :]
        pos = j * tm + r0 + 1 + lax.broadcasted_iota(jnp.int32, (sub, 1), 0)
        pieces = []
        width = 1
        for w in POOL_WINDOWS:
            while width < w:
                s = s + pltpu.roll(s, width, 0)
                width *= 2
            count = jnp.minimum(pos, w).astype(F32)
            pieces.append((s[POOL_HALO:, :gd] / count - p[:, :gd]).astype(BF16))
            if w != POOL_WINDOWS[-1]:
                s = s[:, gd:]
                p = p[:, gd:]
        yield
        e = jnp.concatenate([_dot(pieces[gi], wgrp_ref[gi]) for gi in range(len(POOL_WINDOWS))], axis=-1)
        m = _dot((e * scale_ref[...]).astype(BF16), wout_ref[...])
        o_ref[r0:r0 + sub, :] = x_ref[r0:r0 + sub, :] + _rms(m, post_g_ref[...])

    _run_skewed([chain(r0) for r0 in range(0, tm, sub)], ODD_SKEW)
    pbuf_ref[0:POOL_HALO, :] = pbuf_ref[tm:tm + POOL_HALO, :]


def _odd(h, pre_g, win, wgrp, scale, wout, post_g, seq):
    n, d = h.shape
    g, gd, _ = wgrp.shape
    return pl.pallas_call(
        functools.partial(_odd_body, nb=seq // ROW_TILE),
        out_shape=jax.ShapeDtypeStruct((n, d), F32),
        grid=(n // ROW_TILE,),
        in_specs=[_rows(ROW_TILE, d), _resident((1, d)), _resident((d, d)), _resident((g, gd, gd)),
                  _resident((1, d)), _resident((d, d)), _resident((1, d))],
        out_specs=_rows(ROW_TILE, d),
        scratch_shapes=[pltpu.VMEM((POOL_HALO + ROW_TILE, d), F32)],
        compiler_params=_params(),
        name="pool_mixer",
    )(h, pre_g, win, wgrp, scale, wout, post_g)


def _even_body(x_ref, pre_g_ref, win_ref, convw_ref, convb_ref, clng_ref, clnb_ref, slng_ref, slnb_ref,
               sguw_ref, sgub_ref, wout_ref, post_g_ref, o_ref, zbuf_ref, *, nb):
    tm, d = x_ref.shape
    dc = convw_ref.shape[1]
    ds = d - dc
    width = convw_ref.shape[0]
    hd = ds // SGU_HEADS
    sub = EVEN_SUB_ROWS
    j = pl.program_id(0) % nb

    @pl.when(j == 0)
    def _():
        zbuf_ref[0:CONV_HALO, :] = jnp.zeros((CONV_HALO, dc), F32)

    row = lax.broadcasted_iota(jnp.int32, (SGU_CHUNK, SGU_CHUNK), 0)
    col = lax.broadcasted_iota(jnp.int32, (SGU_CHUNK, SGU_CHUNK), 1)
    w_tril = [jnp.where(col <= row, sguw_ref[hh], 0.0).astype(BF16) for hh in range(SGU_HEADS)]

    def chain(r0):
        hn = _rms(x_ref[r0:r0 + sub, :], pre_g_ref[...]).astype(BF16)
        pa = _dot(hn, win_ref[:, :2 * dc])
        zbuf_ref[CONV_HALO + r0:CONV_HALO + r0 + sub, :] = pa[:, :dc] * jax.nn.sigmoid(pa[:, dc:])
        yield
        pb = _dot(hn, win_ref[:, 2 * dc:])
        ext = zbuf_ref[r0:CONV_HALO + r0 + sub, :]
        n_ext = CONV_HALO + sub
        acc = None
        for b in range(SUBLANES):
            shifted = ext if b == 0 else pltpu.roll(ext, n_ext - b, 0)
            for a in range(CONV_HALO // SUBLANES + 1):
                k = SUBLANES * a + b - (CONV_HALO - (width - 1))
                if 0 <= k < width:
                    term = shifted[SUBLANES * a:SUBLANES * a + sub, :] * convw_ref[k:k + 1, :]
                    acc = term if acc is None else acc + term
        yield
        ya = _layer_norm(acc + convb_ref[...], clng_ref[...], clnb_ref[...])
        ya = (ya * jax.nn.sigmoid(ya)).astype(BF16)
        zb = 0.5 * pb * (1.0 + lax.erf(pb * (2.0 ** -0.5)))
        bu = zb[:, :ds]
        bv = _layer_norm(zb[:, ds:], slng_ref[...], slnb_ref[...]).astype(BF16)
        yb = []
        for hh in range(SGU_HEADS):
            cs = slice(hh * hd, (hh + 1) * hd)
            mixed = [_dot(w_tril[hh], bv[c0:c0 + SGU_CHUNK, cs]) + sgub_ref[hh] for c0 in range(0, sub, SGU_CHUNK)]
            yb.append((bu[:, cs] * jnp.concatenate(mixed, axis=0)).astype(BF16))
        yield
        m = _dot(jnp.concatenate([ya] + yb, axis=-1), wout_ref[...])
        o_ref[r0:r0 + sub, :] = x_ref[r0:r0 + sub, :] + _rms(m, post_g_ref[...])

    _run_skewed([chain(r0) for r0 in range(0, tm, sub)], EVEN_SKEW)
    zbuf_ref[0:CONV_HALO, :] = zbuf_ref[tm:tm + CONV_HALO, :]


def _even(h, pre_g, win, convw, convb, clng, clnb, slng, slnb, sguw, sgub, wout, post_g, seq):
    n, d = h.shape
    width, dc = convw.shape
    ds = d - dc
    hd = ds // SGU_HEADS
    assert width - 1 <= CONV_HALO and EVEN_SUB_ROWS % SGU_CHUNK == 0
    return pl.pallas_call(
        functools.partial(_even_body, nb=seq // ROW_TILE),
        out_shape=jax.ShapeDtypeStruct((n, d), F32),
        grid=(n // ROW_TILE,),
        in_specs=[_rows(ROW_TILE, d), _resident((1, d)), _resident((d, 2 * dc + 2 * ds)),
                  _resident((width, dc)), _resident((1, dc)), _resident((1, dc)), _resident((1, dc)),
                  _resident((1, ds)), _resident((1, ds)),
                  _resident((SGU_HEADS, SGU_CHUNK, SGU_CHUNK)), _resident((SGU_HEADS, SGU_CHUNK, hd)),
                  _resident((d, d)), _resident((1, d))],
        out_specs=_rows(ROW_TILE, d),
        scratch_shapes=[pltpu.VMEM((CONV_HALO + ROW_TILE, dc), F32)],
        compiler_params=_params(),
        name="conv_sgu_mixer",
    )(h, pre_g, win, convw, convb, clng, clnb, slng, slnb, sguw, sgub, wout, post_g)


def kernel(x, mem, ffn1_pre_g, ffn1_w_gu, ffn1_w_down, ffn1_post_g, mix_pre_g, mix_post_g, ev_w_in, ev_conv_w, ev_conv_b, ev_conv_ln_g, ev_conv_ln_b, ev_sgu_ln_g, ev_sgu_ln_b, ev_sgu_w, ev_sgu_b, ev_w_out, od_w_in, od_w_group, od_scale, od_w_out, xa_pre_g, xa_mem_g, xa_w_q, xa_w_kv, xa_w_o, xa_post_g, ffn2_pre_g, ffn2_w_gu, ffn2_w_down, ffn2_post_g):
    bsz, seq, d = x.shape
    depth = ffn1_w_gu.shape[0]
    d_ff = ffn1_w_down.shape[1]
    hd = (d - ev_conv_w.shape[2]) // SGU_HEADS
    assert seq % ROW_TILE == 0 and all(ROW_TILE % r == 0 for r in (FFN_SUB_ROWS, XA_SUB_ROWS, ODD_SUB_ROWS, EVEN_SUB_ROWS))

    def row(v):
        return v.reshape(1, -1)

    h = x.reshape(bsz * seq, d)
    for i in range(depth):
        h = _ffn(h, row(ffn1_pre_g[i]), ffn1_w_gu[i, :, :d_ff].astype(BF16), ffn1_w_gu[i, :, d_ff:].astype(BF16),
                 ffn1_w_down[i].astype(BF16), row(ffn1_post_g[i]))
        if i % 2 == 0:
            e = i // 2
            sgub = jnp.broadcast_to(ev_sgu_b[e][:, :, None], (SGU_HEADS, SGU_CHUNK, hd))
            h = _even(h, row(mix_pre_g[i]), ev_w_in[e].astype(BF16), ev_conv_w[e], row(ev_conv_b[e]),
                      row(ev_conv_ln_g[e]), row(ev_conv_ln_b[e]), row(ev_sgu_ln_g[e]), row(ev_sgu_ln_b[e]),
                      ev_sgu_w[e], sgub, ev_w_out[e].astype(BF16), row(mix_post_g[i]), seq)
        else:
            o = i // 2
            h = _odd(h, row(mix_pre_g[i]), od_w_in[o].astype(BF16), od_w_group[o].astype(BF16), row(od_scale[o]),
                     od_w_out[o].astype(BF16), row(mix_post_g[i]), seq)
        kt, v = _kv(mem, row(xa_mem_g[i]), xa_w_kv[i, :, :d].T.astype(BF16), xa_w_kv[i, :, d:].astype(BF16))
        h = _xa(h, row(xa_pre_g[i]), xa_w_q[i].astype(BF16), kt, v, xa_w_o[i].astype(BF16), row(xa_post_g[i]), seq)
        h = _ffn(h, row(ffn2_pre_g[i]), ffn2_w_gu[i, :, :d_ff].astype(BF16), ffn2_w_gu[i, :, d_ff:].astype(BF16),
                 ffn2_w_down[i].astype(BF16), row(ffn2_post_g[i]))
    return h.reshape(bsz, seq, d)
```

```python
import functools

import jax
import jax.numpy as jnp
from jax import lax
from jax.experimental import pallas as pl
from jax.experimental.pallas import tpu as pltpu

F32 = jnp.float32
BF16 = jnp.bfloat16

EPS = 1e-6
SGU_HEADS = 4
SGU_CHUNK = 128
POOL_WINDOWS = (2, 4, 8, 16)
XA_HEADS = 4

SUBLANES = 8
BF16_SUBLANES = 16
ROW_TILE = 1024
FFN_SUB_ROWS, FFN_SKEW = 256, 1
XA_SUB_ROWS, XA_SKEW = 512, None
ODD_SUB_ROWS, ODD_SKEW = 256, 1
EVEN_SUB_ROWS, EVEN_SKEW = 512, 2
FF_CHUNK = 1024
CONV_HALO = 32
POOL_HALO = 16
VMEM_LIMIT = 56 * 1024 * 1024


def _rms(x, g):
    return x * lax.rsqrt(jnp.mean(x * x, axis=-1, keepdims=True) + EPS) * g


def _layer_norm(x, g, b):
    mu = jnp.mean(x, axis=-1, keepdims=True)
    xc = x - mu
    var = jnp.mean(xc * xc, axis=-1, keepdims=True)
    return xc * lax.rsqrt(var + EPS) * g + b


def _dot(a, b):
    return jnp.dot(a, b, preferred_element_type=F32)


def _resident(shape):
    zeros = (0,) * len(shape)
    return pl.BlockSpec(shape, lambda i: zeros, pipeline_mode=pl.Buffered(1))


def _rows(tm, d):
    return pl.BlockSpec((tm, d), lambda i: (i, 0))


def _params():
    return pltpu.CompilerParams(dimension_semantics=("arbitrary",), vmem_limit_bytes=VMEM_LIMIT)


def _run_skewed(chains, skew):
    if skew is None:
        for chain in chains:
            for _ in chain:
                pass
        return
    done = [False] * len(chains)
    tick = 0
    while not all(done):
        for c, chain in enumerate(chains):
            if not done[c] and tick >= c * skew:
                try:
                    next(chain)
                except StopIteration:
                    done[c] = True
        tick += 1


def _cast_plan(stacked, layer, steps):
    _, r, c = stacked.shape
    br = next(b for b in range(BF16_SUBLANES, r + 1, BF16_SUBLANES) if r % b == 0 and r // b <= steps)
    last = r // br - 1
    in_spec = pl.BlockSpec((None, br, c), lambda i: (layer, jnp.minimum(i, last), 0))
    out_spec = pl.BlockSpec((br, c), lambda i: (jnp.minimum(i, last), 0))
    return in_spec, out_spec, jax.ShapeDtypeStruct((r, c), BF16)


def _ffn_body(x_ref, pre_g_ref, wgu_ref, wd_ref, post_g_ref, *rest, chunks, d_ff, n_cast):
    cast_in, o_ref, cast_out = rest[:n_cast], rest[n_cast], rest[n_cast + 1:]

    def chain(rs):
        hn = _rms(x_ref[rs, :], pre_g_ref[...]).astype(BF16)
        yield
        f = None
        for c0, c1 in chunks:
            g = _dot(hn, wgu_ref[:, c0:c1])
            u = _dot(hn, wgu_ref[:, d_ff + c0:d_ff + c1])
            a = (g * jax.nn.sigmoid(g) * u).astype(BF16)
            fc = _dot(a, wd_ref[c0:c1, :])
            f = fc if f is None else f + fc
            yield
        o_ref[rs, :] = x_ref[rs, :] + 0.5 * _rms(f, post_g_ref[...])

    def casts():
        for src, dst in zip(cast_in, cast_out):
            dst[...] = src[...].astype(BF16)
            yield

    rows = [chain(slice(r0, r0 + FFN_SUB_ROWS)) for r0 in range(0, x_ref.shape[0], FFN_SUB_ROWS)]
    _run_skewed(rows + [casts()], FFN_SKEW)


def _ffn(h, pre_g, wgu, wd, post_g, casts=()):
    n, d = h.shape
    d_ff = wd.shape[0]
    steps = n // ROW_TILE
    chunks = tuple((c, min(c + FF_CHUNK, d_ff)) for c in range(0, d_ff, FF_CHUNK))
    plans = [_cast_plan(stack, layer, steps) for stack, layer in casts]
    outs = pl.pallas_call(
        functools.partial(_ffn_body, chunks=chunks, d_ff=d_ff, n_cast=len(plans)),
        out_shape=[jax.ShapeDtypeStruct((n, d), F32)] + [p[2] for p in plans],
        grid=(steps,),
        in_specs=[_rows(ROW_TILE, d), _resident((1, d)), _resident((d, 2 * d_ff)), _resident((d_ff, d)),
                  _resident((1, d))] + [p[0] for p in plans],
        out_specs=[_rows(ROW_TILE, d)] + [p[1] for p in plans],
        compiler_params=_params(),
        name="swiglu_ffn",
    )(h, pre_g, wgu, wd, post_g, *[stack for stack, _ in casts])
    return outs[0], outs[1:]


def _kv_body(mem_ref, g_ref, wkt_ref, wkv_ref, kt_ref, v_ref):
    d = mem_ref.shape[2]
    memn = _rms(mem_ref[0], g_ref[...]).astype(BF16)
    kt = lax.dot_general(wkt_ref[...], memn, (((1,), (1,)), ((), ())), preferred_element_type=F32)
    kt_ref[0] = kt.astype(BF16)
    v_ref[0] = _dot(memn, wkv_ref[:, d:]).astype(BF16)


def _kv(mem, g, wkt, wkv):
    b, m, d = mem.shape
    return pl.pallas_call(
        _kv_body,
        out_shape=(jax.ShapeDtypeStruct((b, d, m), BF16), jax.ShapeDtypeStruct((b, m, d), BF16)),
        grid=(b,),
        in_specs=[pl.BlockSpec((1, m, d), lambda i: (i, 0, 0)), _resident((1, d)), _resident((d, d)),
                  _resident((d, 2 * d))],
        out_specs=(pl.BlockSpec((1, d, m), lambda i: (i, 0, 0)), pl.BlockSpec((1, m, d), lambda i: (i, 0, 0))),
        compiler_params=_params(),
        name="xattn_kv",
    )(mem, g, wkt, wkv)


def _xa_body(x_ref, pre_g_ref, wq_ref, kt_ref, v_ref, wo_ref, post_g_ref, o_ref, *, heads):
    d = x_ref.shape[1]
    hd = d // heads

    def chain(rs):
        hn = _rms(x_ref[rs, :], pre_g_ref[...]).astype(BF16)
        q = (_dot(hn, wq_ref[...]) * (hd ** -0.5)).astype(BF16)
        yield
        outs = []
        for h in range(heads):
            sl = slice(h * hd, (h + 1) * hd)
            s = _dot(q[:, sl], kt_ref[0, sl, :])
            p = jnp.exp(s - jnp.max(s, axis=-1, keepdims=True))
            l = jnp.sum(p, axis=-1, keepdims=True)
            outs.append((_dot(p.astype(BF16), v_ref[0, :, sl]) / l).astype(BF16))
            yield
        c = _dot(jnp.concatenate(outs, axis=-1), wo_ref[...])
        o_ref[rs, :] = x_ref[rs, :] + _rms(c, post_g_ref[...])

    _run_skewed([chain(slice(r0, r0 + XA_SUB_ROWS)) for r0 in range(0, x_ref.shape[0], XA_SUB_ROWS)], XA_SKEW)


def _xa(h, pre_g, wq, kt, v, wo, post_g, seq):
    n, d = h.shape
    m = v.shape[1]
    nb = seq // ROW_TILE
    return pl.pallas_call(
        functools.partial(_xa_body, heads=XA_HEADS),
        out_shape=jax.ShapeDtypeStruct((n, d), F32),
        grid=(n // ROW_TILE,),
        in_specs=[_rows(ROW_TILE, d), _resident((1, d)), _resident((d, d)),
                  pl.BlockSpec((1, d, m), lambda i: (i // nb, 0, 0)),
                  pl.BlockSpec((1, m, d), lambda i: (i // nb, 0, 0)),
                  _resident((d, d)), _resident((1, d))],
        out_specs=_rows(ROW_TILE, d),
        compiler_params=_params(),
        name="xattn",
    )(h, pre_g, wq, kt, v, wo, post_g)


def _odd_body(x_ref, pre_g_ref, win_ref, wgrp_ref, scale_ref, wout_ref, post_g_ref, o_ref, pbuf_ref, *, nb):
    tm, d = x_ref.shape
    gd = d // len(POOL_WINDOWS)
    sub = ODD_SUB_ROWS
    j = pl.program_id(0) % nb

    @pl.when(j == 0)
    def _():
        pbuf_ref[0:POOL_HALO, :] = jnp.zeros((POOL_HALO, d), F32)

    def chain(r0):
        hn = _rms(x_ref[r0:r0 + sub, :], pre_g_ref[...]).astype(BF16)
        p = _dot(hn, win_ref[...])
        pbuf_ref[POOL_HALO + r0:POOL_HALO + r0 + sub, :] = p
        yield
        s = pbuf_ref[r0:POOL_HALO + r0 + sub, :]
        pos = j * tm + r0 + 1 + lax.broadcasted_iota(jnp.int32, (sub, 1), 0)
        pieces = []
        width = 1
        for w in POOL_WINDOWS:
            while width < w:
                s = s + pltpu.roll(s, width, 0)
                width *= 2
            count = jnp.minimum(pos, w).astype(F32)
            pieces.append((s[POOL_HALO:, :gd] / count - p[:, :gd]).astype(BF16))
            if w != POOL_WINDOWS[-1]:
                s = s[:, gd:]
                p = p[:, gd:]
        yield
        e = jnp.concatenate([_dot(pieces[gi], wgrp_ref[gi]) for gi in range(len(POOL_WINDOWS))], axis=-1)
        m = _dot((e * scale_ref[...]).astype(BF16), wout_ref[...])
        o_ref[r0:r0 + sub, :] = x_ref[r0:r0 + sub, :] + _rms(m, post_g_ref[...])

    _run_skewed([chain(r0) for r0 in range(0, tm, sub)], ODD_SKEW)
    pbuf_ref[0:POOL_HALO, :] = pbuf_ref[tm:tm + POOL_HALO, :]


def _odd(h, pre_g, win, wgrp, scale, wout, post_g, seq):
    n, d = h.shape
    g, gd, _ = wgrp.shape
    return pl.pallas_call(
        functools.partial(_odd_body, nb=seq // ROW_TILE),
        out_shape=jax.ShapeDtypeStruct((n, d), F32),
        grid=(n // ROW_TILE,),
        in_specs=[_rows(ROW_TILE, d), _resident((1, d)), _resident((d, d)), _resident((g, gd, gd)),
                  _resident((1, d)), _resident((d, d)), _resident((1, d))],
        out_specs=_rows(ROW_TILE, d),
        scratch_shapes=[pltpu.VMEM((POOL_HALO + ROW_TILE, d), F32)],
        compiler_params=_params(),
        name="pool_mixer",
    )(h, pre_g, win, wgrp, scale, wout, post_g)


def _even_body(x_ref, pre_g_ref, win_ref, convw_ref, convb_ref, clng_ref, clnb_ref, slng_ref, slnb_ref,
               sguw_ref, sgub_ref, wout_ref, post_g_ref, o_ref, zbuf_ref, *, nb):
    tm, d = x_ref.shape
    dc = convw_ref.shape[1]
    ds = d - dc
    width = convw_ref.shape[0]
    hd = ds // SGU_HEADS
    sub = EVEN_SUB_ROWS
    j = pl.program_id(0) % nb

    @pl.when(j == 0)
    def _():
        zbuf_ref[0:CONV_HALO, :] = jnp.zeros((CONV_HALO, dc), F32)

    row = lax.broadcasted_iota(jnp.int32, (SGU_CHUNK, SGU_CHUNK), 0)
    col = lax.broadcasted_iota(jnp.int32, (SGU_CHUNK, SGU_CHUNK), 1)
    w_tril = [jnp.where(col <= row, sguw_ref[hh], 0.0).astype(BF16) for hh in range(SGU_HEADS)]

    def chain(r0):
        hn = _rms(x_ref[r0:r0 + sub, :], pre_g_ref[...]).astype(BF16)
        pa = _dot(hn, win_ref[:, :2 * dc])
        zbuf_ref[CONV_HALO + r0:CONV_HALO + r0 + sub, :] = pa[:, :dc] * jax.nn.sigmoid(pa[:, dc:])
        yield
        pb = _dot(hn, win_ref[:, 2 * dc:])
        ext = zbuf_ref[r0:CONV_HALO + r0 + sub, :]
        n_ext = CONV_HALO + sub
        acc = None
        for b in range(SUBLANES):
            shifted = ext if b == 0 else pltpu.roll(ext, n_ext - b, 0)
            for a in range(CONV_HALO // SUBLANES + 1):
                k = SUBLANES * a + b - (CONV_HALO - (width - 1))
                if 0 <= k < width:
                    term = shifted[SUBLANES * a:SUBLANES * a + sub, :] * convw_ref[k:k + 1, :]
                    acc = term if acc is None else acc + term
        yield
        ya = _layer_norm(acc + convb_ref[...], clng_ref[...], clnb_ref[...])
        ya = (ya * jax.nn.sigmoid(ya)).astype(BF16)
        zb = 0.5 * pb * (1.0 + lax.erf(pb * (2.0 ** -0.5)))
        bu = zb[:, :ds]
        bv = _layer_norm(zb[:, ds:], slng_ref[...], slnb_ref[...]).astype(BF16)
        yb = []
        for hh in range(SGU_HEADS):
            cs = slice(hh * hd, (hh + 1) * hd)
            mixed = [_dot(w_tril[hh], bv[c0:c0 + SGU_CHUNK, cs]) + sgub_ref[hh] for c0 in range(0, sub, SGU_CHUNK)]
            yb.append((bu[:, cs] * jnp.concatenate(mixed, axis=0)).astype(BF16))
        yield
        m = _dot(jnp.concatenate([ya] + yb, axis=-1), wout_ref[...])
        o_ref[r0:r0 + sub, :] = x_ref[r0:r0 + sub, :] + _rms(m, post_g_ref[...])

    _run_skewed([chain(r0) for r0 in range(0, tm, sub)], EVEN_SKEW)
    zbuf_ref[0:CONV_HALO, :] = zbuf_ref[tm:tm + CONV_HALO, :]


def _even(h, pre_g, win, convw, convb, clng, clnb, slng, slnb, sguw, sgub, wout, post_g, seq):
    n, d = h.shape
    width, dc = convw.shape
    ds = d - dc
    hd = ds // SGU_HEADS
    assert width - 1 <= CONV_HALO and EVEN_SUB_ROWS % SGU_CHUNK == 0
    return pl.pallas_call(
        functools.partial(_even_body, nb=seq // ROW_TILE),
        out_shape=jax.ShapeDtypeStruct((n, d), F32),
        grid=(n // ROW_TILE,),
        in_specs=[_rows(ROW_TILE, d), _resident((1, d)), _resident((d, 2 * dc + 2 * ds)),
                  _resident((width, dc)), _resident((1, dc)), _resident((1, dc)), _resident((1, dc)),
                  _resident((1, ds)), _resident((1, ds)),
                  _resident((SGU_HEADS, SGU_CHUNK, SGU_CHUNK)), _resident((SGU_HEADS, SGU_CHUNK, hd)),
                  _resident((d, d)), _resident((1, d))],
        out_specs=_rows(ROW_TILE, d),
        scratch_shapes=[pltpu.VMEM((CONV_HALO + ROW_TILE, dc), F32)],
        compiler_params=_params(),
        name="conv_sgu_mixer",
    )(h, pre_g, win, convw, convb, clng, clnb, slng, slnb, sguw, sgub, wout, post_g)


def kernel(x, mem, ffn1_pre_g, ffn1_w_gu, ffn1_w_down, ffn1_post_g, mix_pre_g, mix_post_g, ev_w_in, ev_conv_w, ev_conv_b, ev_conv_ln_g, ev_conv_ln_b, ev_sgu_ln_g, ev_sgu_ln_b, ev_sgu_w, ev_sgu_b, ev_w_out, od_w_in, od_w_group, od_scale, od_w_out, xa_pre_g, xa_mem_g, xa_w_q, xa_w_kv, xa_w_o, xa_post_g, ffn2_pre_g, ffn2_w_gu, ffn2_w_down, ffn2_post_g):
    bsz, seq, d = x.shape
    depth = ffn1_w_gu.shape[0]
    hd = (d - ev_conv_w.shape[2]) // SGU_HEADS
    assert seq % ROW_TILE == 0 and all(ROW_TILE % r == 0 for r in (FFN_SUB_ROWS, XA_SUB_ROWS, ODD_SUB_ROWS, EVEN_SUB_ROWS))

    def row(v):
        return v.reshape(1, -1)

    h = x.reshape(bsz * seq, d)
    od_w_group2 = od_w_group.reshape(od_w_group.shape[0], -1, od_w_group.shape[-1])
    wgu, wd = ffn1_w_gu[0].astype(BF16), ffn1_w_down[0].astype(BF16)
    for i in range(depth):
        m = i // 2
        mix = [(ev_w_in, m), (ev_w_out, m)] if i % 2 == 0 else [(od_w_in, m), (od_w_group2, m), (od_w_out, m)]
        h, w = _ffn(h, row(ffn1_pre_g[i]), wgu, wd, row(ffn1_post_g[i]),
                    mix + [(xa_w_q, i), (xa_w_kv, i), (xa_w_o, i), (ffn2_w_gu, i), (ffn2_w_down, i)])
        *w_mix, wq, wkv, wo, wgu, wd = w
        if i % 2 == 0:
            sgub = jnp.broadcast_to(ev_sgu_b[m][:, :, None], (SGU_HEADS, SGU_CHUNK, hd))
            h = _even(h, row(mix_pre_g[i]), w_mix[0], ev_conv_w[m], row(ev_conv_b[m]),
                      row(ev_conv_ln_g[m]), row(ev_conv_ln_b[m]), row(ev_sgu_ln_g[m]), row(ev_sgu_ln_b[m]),
                      ev_sgu_w[m], sgub, w_mix[1], row(mix_post_g[i]), seq)
        else:
            h = _odd(h, row(mix_pre_g[i]), w_mix[0], w_mix[1].reshape(od_w_group.shape[1:]), row(od_scale[m]),
                     w_mix[2], row(mix_post_g[i]), seq)
        kt, v = _kv(mem, row(xa_mem_g[i]), wkv[:, :d].T, wkv)
        h = _xa(h, row(xa_pre_g[i]), wq, kt, v, wo, row(xa_post_g[i]), seq)
        nxt = [(ffn1_w_gu, i + 1), (ffn1_w_down, i + 1)] if i + 1 < depth else []
        h, w = _ffn(h, row(ffn2_pre_g[i]), wgu, wd, row(ffn2_post_g[i]), nxt)
        if nxt:
            wgu, wd = w
    return h.reshape(bsz, seq, d)
```

```python
import functools

import jax
import jax.numpy as jnp
from jax import lax
from jax.experimental import pallas as pl
from jax.experimental.pallas import tpu as pltpu

F32 = jnp.float32
BF16 = jnp.bfloat16

EPS = 1e-6
SGU_HEADS = 4
SGU_CHUNK = 128
POOL_WINDOWS = (2, 4, 8, 16)
XA_HEADS = 4

SUBLANES = 8
BF16_SUBLANES = 16
ROW_TILE = 1024
FFN_SUB_ROWS, FFN_SKEW = 256, 1
XA_SUB_ROWS, XA_SKEW = 512, None
ODD_SUB_ROWS, ODD_SKEW = 256, 1
EVEN_SUB_ROWS, EVEN_SKEW = 512, 1
FF_CHUNK = 1024
MXU_COLS = 256
CONV_STRIP = 128
CONV_HALO = 32
POOL_HALO = 16
VMEM_LIMIT = 56 * 1024 * 1024


def _rms(x, g):
    return x * lax.rsqrt(jnp.mean(x * x, axis=-1, keepdims=True) + EPS) * g


def _layer_norm(x, g, b):
    mu = jnp.mean(x, axis=-1, keepdims=True)
    xc = x - mu
    var = jnp.mean(xc * xc, axis=-1, keepdims=True)
    return xc * lax.rsqrt(var + EPS) * g + b


def _dot(a, b):
    return jnp.dot(a, b, preferred_element_type=F32)


def _resident(shape):
    zeros = (0,) * len(shape)
    return pl.BlockSpec(shape, lambda i: zeros, pipeline_mode=pl.Buffered(1))


def _rows(tm, d):
    return pl.BlockSpec((tm, d), lambda i: (i, 0))


def _params():
    return pltpu.CompilerParams(dimension_semantics=("arbitrary",), vmem_limit_bytes=VMEM_LIMIT)


MICRO = "micro"


def _run_skewed(chains, skew):
    if skew is None:
        for chain in chains:
            for _ in chain:
                pass
        return
    done = [False] * len(chains)
    tick = 0
    while not all(done):
        active = [c for c in range(len(chains)) if not done[c] and tick >= c * skew]
        while active:
            for c in list(active):
                try:
                    if next(chains[c]) is not MICRO:
                        active.remove(c)
                except StopIteration:
                    done[c] = True
                    active.remove(c)
        tick += 1


def _tiled_dot(lhs, w_ref, c0, c1):
    tiles = []
    for t0 in range(c0, c1, MXU_COLS):
        tiles.append(_dot(lhs, w_ref[:, t0:min(t0 + MXU_COLS, c1)]))
        yield MICRO
    return jnp.concatenate(tiles, axis=-1)


def _cast_plan(stacked, layer, steps):
    _, r, c = stacked.shape
    br = next(b for b in range(BF16_SUBLANES, r + 1, BF16_SUBLANES) if r % b == 0 and r // b <= steps)
    last = r // br - 1
    in_spec = pl.BlockSpec((None, br, c), lambda i: (layer, jnp.minimum(i, last), 0))
    out_spec = pl.BlockSpec((br, c), lambda i: (jnp.minimum(i, last), 0))
    return in_spec, out_spec, jax.ShapeDtypeStruct((r, c), BF16)


def _ffn_body(x_ref, pre_g_ref, wgu_ref, wd_ref, post_g_ref, *rest, chunks, d_ff, n_cast):
    cast_in, o_ref, cast_out = rest[:n_cast], rest[n_cast], rest[n_cast + 1:]

    def chain(rs):
        hn = _rms(x_ref[rs, :], pre_g_ref[...]).astype(BF16)
        yield
        f = None
        for c0, c1 in chunks:
            g = _dot(hn, wgu_ref[:, c0:c1])
            u = _dot(hn, wgu_ref[:, d_ff + c0:d_ff + c1])
            a = (g * jax.nn.sigmoid(g) * u).astype(BF16)
            fc = _dot(a, wd_ref[c0:c1, :])
            f = fc if f is None else f + fc
            yield
        o_ref[rs, :] = x_ref[rs, :] + 0.5 * _rms(f, post_g_ref[...])

    def casts():
        for src, dst in zip(cast_in, cast_out):
            dst[...] = src[...].astype(BF16)
            yield MICRO

    rows = [chain(slice(r0, r0 + FFN_SUB_ROWS)) for r0 in range(0, x_ref.shape[0], FFN_SUB_ROWS)]
    _run_skewed(rows + [casts()], FFN_SKEW)


def _ffn(h, pre_g, wgu, wd, post_g, casts=()):
    n, d = h.shape
    d_ff = wd.shape[0]
    steps = n // ROW_TILE
    chunks = tuple((c, min(c + FF_CHUNK, d_ff)) for c in range(0, d_ff, FF_CHUNK))
    plans = [_cast_plan(stack, layer, steps) for stack, layer in casts]
    outs = pl.pallas_call(
        functools.partial(_ffn_body, chunks=chunks, d_ff=d_ff, n_cast=len(plans)),
        out_shape=[jax.ShapeDtypeStruct((n, d), F32)] + [p[2] for p in plans],
        grid=(steps,),
        in_specs=[_rows(ROW_TILE, d), _resident((1, d)), _resident((d, 2 * d_ff)), _resident((d_ff, d)),
                  _resident((1, d))] + [p[0] for p in plans],
        out_specs=[_rows(ROW_TILE, d)] + [p[1] for p in plans],
        compiler_params=_params(),
        name="swiglu_ffn",
    )(h, pre_g, wgu, wd, post_g, *[stack for stack, _ in casts])
    return outs[0], outs[1:]


def _kv_body(mem_ref, g_ref, wkt_ref, wkv_ref, kt_ref, v_ref):
    d = mem_ref.shape[2]
    memn = _rms(mem_ref[0], g_ref[...]).astype(BF16)
    kt = lax.dot_general(wkt_ref[...], memn, (((1,), (1,)), ((), ())), preferred_element_type=F32)
    kt_ref[0] = kt.astype(BF16)
    v_ref[0] = _dot(memn, wkv_ref[:, d:]).astype(BF16)


def _kv(mem, g, wkt, wkv):
    b, m, d = mem.shape
    return pl.pallas_call(
        _kv_body,
        out_shape=(jax.ShapeDtypeStruct((b, d, m), BF16), jax.ShapeDtypeStruct((b, m, d), BF16)),
        grid=(b,),
        in_specs=[pl.BlockSpec((1, m, d), lambda i: (i, 0, 0)), _resident((1, d)), _resident((d, d)),
                  _resident((d, 2 * d))],
        out_specs=(pl.BlockSpec((1, d, m), lambda i: (i, 0, 0)), pl.BlockSpec((1, m, d), lambda i: (i, 0, 0))),
        compiler_params=_params(),
        name="xattn_kv",
    )(mem, g, wkt, wkv)


def _xa_body(x_ref, pre_g_ref, wq_ref, kt_ref, v_ref, wo_ref, post_g_ref, o_ref, *, heads):
    d = x_ref.shape[1]
    hd = d // heads

    def chain(rs):
        hn = _rms(x_ref[rs, :], pre_g_ref[...]).astype(BF16)
        q = (_dot(hn, wq_ref[...]) * (hd ** -0.5)).astype(BF16)
        yield
        outs = []
        for h in range(heads):
            sl = slice(h * hd, (h + 1) * hd)
            s = _dot(q[:, sl], kt_ref[0, sl, :])
            p = jnp.exp(s - jnp.max(s, axis=-1, keepdims=True))
            l = jnp.sum(p, axis=-1, keepdims=True)
            outs.append((_dot(p.astype(BF16), v_ref[0, :, sl]) / l).astype(BF16))
            yield
        c = _dot(jnp.concatenate(outs, axis=-1), wo_ref[...])
        o_ref[rs, :] = x_ref[rs, :] + _rms(c, post_g_ref[...])

    _run_skewed([chain(slice(r0, r0 + XA_SUB_ROWS)) for r0 in range(0, x_ref.shape[0], XA_SUB_ROWS)], XA_SKEW)


def _xa(h, pre_g, wq, kt, v, wo, post_g, seq):
    n, d = h.shape
    m = v.shape[1]
    nb = seq // ROW_TILE
    return pl.pallas_call(
        functools.partial(_xa_body, heads=XA_HEADS),
        out_shape=jax.ShapeDtypeStruct((n, d), F32),
        grid=(n // ROW_TILE,),
        in_specs=[_rows(ROW_TILE, d), _resident((1, d)), _resident((d, d)),
                  pl.BlockSpec((1, d, m), lambda i: (i // nb, 0, 0)),
                  pl.BlockSpec((1, m, d), lambda i: (i // nb, 0, 0)),
                  _resident((d, d)), _resident((1, d))],
        out_specs=_rows(ROW_TILE, d),
        compiler_params=_params(),
        name="xattn",
    )(h, pre_g, wq, kt, v, wo, post_g)


def _odd_body(x_ref, pre_g_ref, win_ref, wgrp_ref, scale_ref, wout_ref, post_g_ref, o_ref, pbuf_ref, *, nb):
    tm, d = x_ref.shape
    gd = d // len(POOL_WINDOWS)
    sub = ODD_SUB_ROWS
    j = pl.program_id(0) % nb

    @pl.when(j == 0)
    def _():
        pbuf_ref[0:POOL_HALO, :] = jnp.zeros((POOL_HALO, d), F32)

    def chain(r0):
        hn = _rms(x_ref[r0:r0 + sub, :], pre_g_ref[...]).astype(BF16)
        p = yield from _tiled_dot(hn, win_ref, 0, d)
        pbuf_ref[POOL_HALO + r0:POOL_HALO + r0 + sub, :] = p
        yield
        s = pbuf_ref[r0:POOL_HALO + r0 + sub, :]
        pos = j * tm + r0 + 1 + lax.broadcasted_iota(jnp.int32, (sub, 1), 0)
        pieces = []
        width = 1
        for w in POOL_WINDOWS:
            while width < w:
                s = s + pltpu.roll(s, width, 0)
                width *= 2
            count = jnp.minimum(pos, w).astype(F32)
            pieces.append((s[POOL_HALO:, :gd] / count - p[:, :gd]).astype(BF16))
            if w != POOL_WINDOWS[-1]:
                s = s[:, gd:]
                p = p[:, gd:]
            yield MICRO
        yield
        e = []
        for gi in range(len(POOL_WINDOWS)):
            e.append(_dot(pieces[gi], wgrp_ref[gi]))
            yield MICRO
        e = (jnp.concatenate(e, axis=-1) * scale_ref[...]).astype(BF16)
        m = yield from _tiled_dot(e, wout_ref, 0, d)
        o_ref[r0:r0 + sub, :] = x_ref[r0:r0 + sub, :] + _rms(m, post_g_ref[...])

    _run_skewed([chain(r0) for r0 in range(0, tm, sub)], ODD_SKEW)
    pbuf_ref[0:POOL_HALO, :] = pbuf_ref[tm:tm + POOL_HALO, :]


def _odd(h, pre_g, win, wgrp, scale, wout, post_g, seq):
    n, d = h.shape
    g, gd, _ = wgrp.shape
    return pl.pallas_call(
        functools.partial(_odd_body, nb=seq // ROW_TILE),
        out_shape=jax.ShapeDtypeStruct((n, d), F32),
        grid=(n // ROW_TILE,),
        in_specs=[_rows(ROW_TILE, d), _resident((1, d)), _resident((d, d)), _resident((g, gd, gd)),
                  _resident((1, d)), _resident((d, d)), _resident((1, d))],
        out_specs=_rows(ROW_TILE, d),
        scratch_shapes=[pltpu.VMEM((POOL_HALO + ROW_TILE, d), F32)],
        compiler_params=_params(),
        name="pool_mixer",
    )(h, pre_g, win, wgrp, scale, wout, post_g)


def _even_body(x_ref, pre_g_ref, win_ref, convw_ref, convb_ref, clng_ref, clnb_ref, slng_ref, slnb_ref,
               sguw_ref, sgub_ref, wout_ref, post_g_ref, o_ref, zbuf_ref, *, nb):
    tm, d = x_ref.shape
    dc = convw_ref.shape[1]
    ds = d - dc
    width = convw_ref.shape[0]
    hd = ds // SGU_HEADS
    sub = EVEN_SUB_ROWS
    j = pl.program_id(0) % nb

    @pl.when(j == 0)
    def _():
        zbuf_ref[0:CONV_HALO, :] = jnp.zeros((CONV_HALO, dc), F32)

    row = lax.broadcasted_iota(jnp.int32, (SGU_CHUNK, SGU_CHUNK), 0)
    col = lax.broadcasted_iota(jnp.int32, (SGU_CHUNK, SGU_CHUNK), 1)
    w_tril = [jnp.where(col <= row, sguw_ref[hh], 0.0).astype(BF16) for hh in range(SGU_HEADS)]

    def chain(r0):
        hn = _rms(x_ref[r0:r0 + sub, :], pre_g_ref[...]).astype(BF16)
        pa = yield from _tiled_dot(hn, win_ref, 0, 2 * dc)
        zbuf_ref[CONV_HALO + r0:CONV_HALO + r0 + sub, :] = pa[:, :dc] * jax.nn.sigmoid(pa[:, dc:])
        yield
        strips = []
        for t0 in range(0, sub, CONV_STRIP):
            ext = zbuf_ref[r0 + t0:r0 + t0 + CONV_STRIP + CONV_HALO, :]
            n_ext = CONV_STRIP + CONV_HALO
            acc = None
            for b in range(SUBLANES):
                shifted = ext if b == 0 else pltpu.roll(ext, n_ext - b, 0)
                for a in range(CONV_HALO // SUBLANES + 1):
                    k = SUBLANES * a + b - (CONV_HALO - (width - 1))
                    if 0 <= k < width:
                        term = shifted[SUBLANES * a:SUBLANES * a + CONV_STRIP, :] * convw_ref[k:k + 1, :]
                        acc = term if acc is None else acc + term
            strips.append(acc)
            yield MICRO
        acc = jnp.concatenate(strips, axis=0)
        yield
        pb = yield from _tiled_dot(hn, win_ref, 2 * dc, 2 * dc + 2 * ds)
        ya = _layer_norm(acc + convb_ref[...], clng_ref[...], clnb_ref[...])
        ya = (ya * jax.nn.sigmoid(ya)).astype(BF16)
        yield
        zb = 0.5 * pb * (1.0 + lax.erf(pb * (2.0 ** -0.5)))
        bu = zb[:, :ds]
        bv = _layer_norm(zb[:, ds:], slng_ref[...], slnb_ref[...]).astype(BF16)
        yb = []
        for hh in range(SGU_HEADS):
            cs = slice(hh * hd, (hh + 1) * hd)
            mixed = [_dot(w_tril[hh], bv[c0:c0 + SGU_CHUNK, cs]) + sgub_ref[hh] for c0 in range(0, sub, SGU_CHUNK)]
            yb.append((bu[:, cs] * jnp.concatenate(mixed, axis=0)).astype(BF16))
            yield MICRO
        yield
        m = yield from _tiled_dot(jnp.concatenate([ya] + yb, axis=-1), wout_ref, 0, d)
        o_ref[r0:r0 + sub, :] = x_ref[r0:r0 + sub, :] + _rms(m, post_g_ref[...])

    _run_skewed([chain(r0) for r0 in range(0, tm, sub)], EVEN_SKEW)
    zbuf_ref[0:CONV_HALO, :] = zbuf_ref[tm:tm + CONV_HALO, :]


def _even(h, pre_g, win, convw, convb, clng, clnb, slng, slnb, sguw, sgub, wout, post_g, seq):
    n, d = h.shape
    width, dc = convw.shape
    ds = d - dc
    hd = ds // SGU_HEADS
    assert width - 1 <= CONV_HALO and EVEN_SUB_ROWS % SGU_CHUNK == 0
    return pl.pallas_call(
        functools.partial(_even_body, nb=seq // ROW_TILE),
        out_shape=jax.ShapeDtypeStruct((n, d), F32),
        grid=(n // ROW_TILE,),
        in_specs=[_rows(ROW_TILE, d), _resident((1, d)), _resident((d, 2 * dc + 2 * ds)),
                  _resident((width, dc)), _resident((1, dc)), _resident((1, dc)), _resident((1, dc)),
                  _resident((1, ds)), _resident((1, ds)),
                  _resident((SGU_HEADS, SGU_CHUNK, SGU_CHUNK)), _resident((SGU_HEADS, SGU_CHUNK, hd)),
                  _resident((d, d)), _resident((1, d))],
        out_specs=_rows(ROW_TILE, d),
        scratch_shapes=[pltpu.VMEM((CONV_HALO + ROW_TILE, dc), F32)],
        compiler_params=_params(),
        name="conv_sgu_mixer",
    )(h, pre_g, win, convw, convb, clng, clnb, slng, slnb, sguw, sgub, wout, post_g)


def kernel(x, mem, ffn1_pre_g, ffn1_w_gu, ffn1_w_down, ffn1_post_g, mix_pre_g, mix_post_g, ev_w_in, ev_conv_w, ev_conv_b, ev_conv_ln_g, ev_conv_ln_b, ev_sgu_ln_g, ev_sgu_ln_b, ev_sgu_w, ev_sgu_b, ev_w_out, od_w_in, od_w_group, od_scale, od_w_out, xa_pre_g, xa_mem_g, xa_w_q, xa_w_kv, xa_w_o, xa_post_g, ffn2_pre_g, ffn2_w_gu, ffn2_w_down, ffn2_post_g):
    bsz, seq, d = x.shape
    depth = ffn1_w_gu.shape[0]
    hd = (d - ev_conv_w.shape[2]) // SGU_HEADS
    assert seq % ROW_TILE == 0 and all(ROW_TILE % r == 0 for r in (FFN_SUB_ROWS, XA_SUB_ROWS, ODD_SUB_ROWS, EVEN_SUB_ROWS))

    def row(v):
        return v.reshape(1, -1)

    h = x.reshape(bsz * seq, d)
    od_w_group2 = od_w_group.reshape(od_w_group.shape[0], -1, od_w_group.shape[-1])
    wgu, wd = ffn1_w_gu[0].astype(BF16), ffn1_w_down[0].astype(BF16)
    for i in range(depth):
        m = i // 2
        mix = [(ev_w_in, m), (ev_w_out, m)] if i % 2 == 0 else [(od_w_in, m), (od_w_group2, m), (od_w_out, m)]
        h, w = _ffn(h, row(ffn1_pre_g[i]), wgu, wd, row(ffn1_post_g[i]),
                    mix + [(xa_w_q, i), (xa_w_kv, i), (xa_w_o, i), (ffn2_w_gu, i), (ffn2_w_down, i)])
        *w_mix, wq, wkv, wo, wgu, wd = w
        if i % 2 == 0:
            sgub = jnp.broadcast_to(ev_sgu_b[m][:, :, None], (SGU_HEADS, SGU_CHUNK, hd))
            h = _even(h, row(mix_pre_g[i]), w_mix[0], ev_conv_w[m], row(ev_conv_b[m]),
                      row(ev_conv_ln_g[m]), row(ev_conv_ln_b[m]), row(ev_sgu_ln_g[m]), row(ev_sgu_ln_b[m]),
                      ev_sgu_w[m], sgub, w_mix[1], row(mix_post_g[i]), seq)
        else:
            h = _odd(h, row(mix_pre_g[i]), w_mix[0], w_mix[1].reshape(od_w_group.shape[1:]), row(od_scale[m]),
                     w_mix[2], row(mix_post_g[i]), seq)
        kt, v = _kv(mem, row(xa_mem_g[i]), wkv[:, :d].T, wkv)
        h = _xa(h, row(xa_pre_g[i]), wq, kt, v, wo, row(xa_post_g[i]), seq)
        nxt = [(ffn1_w_gu, i + 1), (ffn1_w_down, i + 1)] if i + 1 < depth else []
        h, w = _ffn(h, row(ffn2_pre_g[i]), wgu, wd, row(ffn2_post_g[i]), nxt)
        if nxt:
            wgu, wd = w
    return h.reshape(bsz, seq, d)
```

```python
import functools

import jax
import jax.numpy as jnp
from jax import lax
from jax.experimental import pallas as pl
from jax.experimental.pallas import tpu as pltpu

F32 = jnp.float32
BF16 = jnp.bfloat16

EPS = 1e-6
SGU_HEADS = 4
SGU_CHUNK = 128
POOL_WINDOWS = (2, 4, 8, 16)
XA_HEADS = 4

SUBLANES = 8
BF16_SUBLANES = 16
ROW_TILE = 1024
XA_ROW_TILE = 2048
ODD_ROW_TILE = 2048
FFN_SUB_ROWS, FFN_SKEW = 256, 1
XA_SUB_ROWS, XA_SKEW = 512, None
ODD_SUB_ROWS, ODD_SKEW = 256, 1
EVEN_SUB_ROWS, EVEN_SKEW = 512, 1
FF_CHUNK = 1024
MXU_COLS = 256
CONV_STRIP = 128
CONV_HALO = 32
POOL_HALO = 16
VMEM_LIMIT = 56 * 1024 * 1024


def _rms(x, g):
    return x * lax.rsqrt(jnp.mean(x * x, axis=-1, keepdims=True) + EPS) * g


def _layer_norm(x, g, b):
    mu = jnp.mean(x, axis=-1, keepdims=True)
    xc = x - mu
    var = jnp.mean(xc * xc, axis=-1, keepdims=True)
    return xc * lax.rsqrt(var + EPS) * g + b


def _dot(a, b):
    return jnp.dot(a, b, preferred_element_type=F32)


def _resident(shape):
    zeros = (0,) * len(shape)
    return pl.BlockSpec(shape, lambda i: zeros, pipeline_mode=pl.Buffered(1))


def _rows(tm, d):
    return pl.BlockSpec((tm, d), lambda i: (i, 0))


def _params():
    return pltpu.CompilerParams(dimension_semantics=("arbitrary",), vmem_limit_bytes=VMEM_LIMIT)


MICRO = "micro"


def _run_skewed(chains, skew):
    if skew is None:
        for chain in chains:
            for _ in chain:
                pass
        return
    done = [False] * len(chains)
    tick = 0
    while not all(done):
        active = [c for c in range(len(chains)) if not done[c] and tick >= c * skew]
        while active:
            for c in list(active):
                try:
                    if next(chains[c]) is not MICRO:
                        active.remove(c)
                except StopIteration:
                    done[c] = True
                    active.remove(c)
        tick += 1


def _tiled_dot(lhs, w_ref, c0, c1):
    tiles = []
    for t0 in range(c0, c1, MXU_COLS):
        tiles.append(_dot(lhs, w_ref[:, t0:min(t0 + MXU_COLS, c1)]))
        yield MICRO
    return jnp.concatenate(tiles, axis=-1)


def _cast_plan(stacked, layer, steps):
    _, r, c = stacked.shape
    br = next(b for b in range(BF16_SUBLANES, r + 1, BF16_SUBLANES) if r % b == 0 and r // b <= steps)
    last = r // br - 1
    in_spec = pl.BlockSpec((None, br, c), lambda i: (layer, jnp.minimum(i, last), 0))
    out_spec = pl.BlockSpec((br, c), lambda i: (jnp.minimum(i, last), 0))
    return in_spec, out_spec, jax.ShapeDtypeStruct((r, c), BF16)


def _ffn_body(x_ref, pre_g_ref, wgu_ref, wd_ref, post_g_ref, *rest, chunks, d_ff, n_cast):
    cast_in, o_ref, cast_out = rest[:n_cast], rest[n_cast], rest[n_cast + 1:]

    def chain(rs):
        hn = _rms(x_ref[rs, :], pre_g_ref[...]).astype(BF16)
        yield
        f = None
        for c0, c1 in chunks:
            g = _dot(hn, wgu_ref[:, c0:c1])
            u = _dot(hn, wgu_ref[:, d_ff + c0:d_ff + c1])
            a = (g * jax.nn.sigmoid(g) * u).astype(BF16)
            fc = _dot(a, wd_ref[c0:c1, :])
            f = fc if f is None else f + fc
            yield
        o_ref[rs, :] = x_ref[rs, :] + 0.5 * _rms(f, post_g_ref[...])

    def casts():
        for src, dst in zip(cast_in, cast_out):
            dst[...] = src[...].astype(BF16)
            yield MICRO

    rows = [chain(slice(r0, r0 + FFN_SUB_ROWS)) for r0 in range(0, x_ref.shape[0], FFN_SUB_ROWS)]
    _run_skewed(rows + [casts()], FFN_SKEW)


def _ffn(h, pre_g, wgu, wd, post_g, casts=()):
    n, d = h.shape
    d_ff = wd.shape[0]
    steps = n // ROW_TILE
    chunks = tuple((c, min(c + FF_CHUNK, d_ff)) for c in range(0, d_ff, FF_CHUNK))
    plans = [_cast_plan(stack, layer, steps) for stack, layer in casts]
    outs = pl.pallas_call(
        functools.partial(_ffn_body, chunks=chunks, d_ff=d_ff, n_cast=len(plans)),
        out_shape=[jax.ShapeDtypeStruct((n, d), F32)] + [p[2] for p in plans],
        grid=(steps,),
        in_specs=[_rows(ROW_TILE, d), _resident((1, d)), _resident((d, 2 * d_ff)), _resident((d_ff, d)),
                  _resident((1, d))] + [p[0] for p in plans],
        out_specs=[_rows(ROW_TILE, d)] + [p[1] for p in plans],
        compiler_params=_params(),
        name="swiglu_ffn",
    )(h, pre_g, wgu, wd, post_g, *[stack for stack, _ in casts])
    return outs[0], outs[1:]


def _kv_body(mem_ref, g_ref, wkv_ref, k_ref, v_ref):
    d = mem_ref.shape[2]
    memn = _rms(mem_ref[0], g_ref[...]).astype(BF16)
    k_ref[0] = _dot(memn, wkv_ref[:, :d]).astype(BF16)
    v_ref[0] = _dot(memn, wkv_ref[:, d:]).astype(BF16)


def _kv(mem, g, wkv):
    b, m, d = mem.shape
    return pl.pallas_call(
        _kv_body,
        out_shape=(jax.ShapeDtypeStruct((b, m, d), BF16), jax.ShapeDtypeStruct((b, m, d), BF16)),
        grid=(b,),
        in_specs=[pl.BlockSpec((1, m, d), lambda i: (i, 0, 0)), _resident((1, d)), _resident((d, 2 * d))],
        out_specs=(pl.BlockSpec((1, m, d), lambda i: (i, 0, 0)), pl.BlockSpec((1, m, d), lambda i: (i, 0, 0))),
        compiler_params=_params(),
        name="xattn_kv",
    )(mem, g, wkv)


def _xa_body(x_ref, pre_g_ref, wq_ref, k_ref, v_ref, wo_ref, post_g_ref, o_ref, *, heads):
    d = x_ref.shape[1]
    hd = d // heads

    def chain(rs):
        hn = _rms(x_ref[rs, :], pre_g_ref[...]).astype(BF16)
        q = (_dot(hn, wq_ref[...]) * (hd ** -0.5)).astype(BF16)
        yield
        outs = []
        for h in range(heads):
            sl = slice(h * hd, (h + 1) * hd)
            s = lax.dot_general(q[:, sl], k_ref[0, :, sl], (((1,), (1,)), ((), ())),
                                preferred_element_type=F32)
            p = jnp.exp(s - jnp.max(s, axis=-1, keepdims=True))
            l = jnp.sum(p, axis=-1, keepdims=True)
            outs.append((_dot(p.astype(BF16), v_ref[0, :, sl]) / l).astype(BF16))
            yield
        c = _dot(jnp.concatenate(outs, axis=-1), wo_ref[...])
        o_ref[rs, :] = x_ref[rs, :] + _rms(c, post_g_ref[...])

    _run_skewed([chain(slice(r0, r0 + XA_SUB_ROWS)) for r0 in range(0, x_ref.shape[0], XA_SUB_ROWS)], XA_SKEW)


def _xa(h, pre_g, wq, k, v, wo, post_g, seq):
    n, d = h.shape
    m = v.shape[1]
    nb = seq // XA_ROW_TILE
    return pl.pallas_call(
        functools.partial(_xa_body, heads=XA_HEADS),
        out_shape=jax.ShapeDtypeStruct((n, d), F32),
        grid=(n // XA_ROW_TILE,),
        in_specs=[_rows(XA_ROW_TILE, d), _resident((1, d)), _resident((d, d)),
                  pl.BlockSpec((1, m, d), lambda i: (i // nb, 0, 0)),
                  pl.BlockSpec((1, m, d), lambda i: (i // nb, 0, 0)),
                  _resident((d, d)), _resident((1, d))],
        out_specs=_rows(XA_ROW_TILE, d),
        compiler_params=_params(),
        name="xattn",
    )(h, pre_g, wq, k, v, wo, post_g)


def _odd_body(x_ref, pre_g_ref, win_ref, wgrp_ref, scale_ref, wout_ref, post_g_ref, o_ref, pbuf_ref, *, nb):
    tm, d = x_ref.shape
    gd = d // len(POOL_WINDOWS)
    sub = ODD_SUB_ROWS
    j = pl.program_id(0) % nb

    @pl.when(j == 0)
    def _():
        pbuf_ref[0:POOL_HALO, :] = jnp.zeros((POOL_HALO, d), F32)

    def chain(r0):
        hn = _rms(x_ref[r0:r0 + sub, :], pre_g_ref[...]).astype(BF16)
        p = yield from _tiled_dot(hn, win_ref, 0, d)
        pbuf_ref[POOL_HALO + r0:POOL_HALO + r0 + sub, :] = p
        yield
        s = pbuf_ref[r0:POOL_HALO + r0 + sub, :]
        pos = j * tm + r0 + 1 + lax.broadcasted_iota(jnp.int32, (sub, 1), 0)
        pieces = []
        width = 1
        for w in POOL_WINDOWS:
            while width < w:
                s = s + pltpu.roll(s, width, 0)
                width *= 2
            count = jnp.minimum(pos, w).astype(F32)
            pieces.append((s[POOL_HALO:, :gd] / count - p[:, :gd]).astype(BF16))
            if w != POOL_WINDOWS[-1]:
                s = s[:, gd:]
                p = p[:, gd:]
            yield MICRO
        yield
        e = []
        for gi in range(len(POOL_WINDOWS)):
            e.append(_dot(pieces[gi], wgrp_ref[gi]))
            yield MICRO
        e = (jnp.concatenate(e, axis=-1) * scale_ref[...]).astype(BF16)
        m = yield from _tiled_dot(e, wout_ref, 0, d)
        o_ref[r0:r0 + sub, :] = x_ref[r0:r0 + sub, :] + _rms(m, post_g_ref[...])

    _run_skewed([chain(r0) for r0 in range(0, tm, sub)], ODD_SKEW)
    pbuf_ref[0:POOL_HALO, :] = pbuf_ref[tm:tm + POOL_HALO, :]


def _odd(h, pre_g, win, wgrp, scale, wout, post_g, seq):
    n, d = h.shape
    g, gd, _ = wgrp.shape
    return pl.pallas_call(
        functools.partial(_odd_body, nb=seq // ODD_ROW_TILE),
        out_shape=jax.ShapeDtypeStruct((n, d), F32),
        grid=(n // ODD_ROW_TILE,),
        in_specs=[_rows(ODD_ROW_TILE, d), _resident((1, d)), _resident((d, d)), _resident((g, gd, gd)),
                  _resident((1, d)), _resident((d, d)), _resident((1, d))],
        out_specs=_rows(ODD_ROW_TILE, d),
        scratch_shapes=[pltpu.VMEM((POOL_HALO + ODD_ROW_TILE, d), F32)],
        compiler_params=_params(),
        name="pool_mixer",
    )(h, pre_g, win, wgrp, scale, wout, post_g)


def _even_body(x_ref, pre_g_ref, win_ref, convw_ref, convb_ref, clng_ref, clnb_ref, slng_ref, slnb_ref,
               sguw_ref, sgub_ref, wout_ref, post_g_ref, o_ref, zbuf_ref, *, nb):
    tm, d = x_ref.shape
    dc = convw_ref.shape[1]
    ds = d - dc
    width = convw_ref.shape[0]
    hd = ds // SGU_HEADS
    sub = EVEN_SUB_ROWS
    j = pl.program_id(0) % nb

    @pl.when(j == 0)
    def _():
        zbuf_ref[0:CONV_HALO, :] = jnp.zeros((CONV_HALO, dc), F32)

    row = lax.broadcasted_iota(jnp.int32, (SGU_CHUNK, SGU_CHUNK), 0)
    col = lax.broadcasted_iota(jnp.int32, (SGU_CHUNK, SGU_CHUNK), 1)
    w_tril = [jnp.where(col <= row, sguw_ref[hh], 0.0).astype(BF16) for hh in range(SGU_HEADS)]

    def chain(r0):
        hn = _rms(x_ref[r0:r0 + sub, :], pre_g_ref[...]).astype(BF16)
        pa = yield from _tiled_dot(hn, win_ref, 0, 2 * dc)
        zbuf_ref[CONV_HALO + r0:CONV_HALO + r0 + sub, :] = pa[:, :dc] * jax.nn.sigmoid(pa[:, dc:])
        yield
        strips = []
        for t0 in range(0, sub, CONV_STRIP):
            ext = zbuf_ref[r0 + t0:r0 + t0 + CONV_STRIP + CONV_HALO, :]
            n_ext = CONV_STRIP + CONV_HALO
            acc = None
            for b in range(SUBLANES):
                shifted = ext if b == 0 else pltpu.roll(ext, n_ext - b, 0)
                for a in range(CONV_HALO // SUBLANES + 1):
                    k = SUBLANES * a + b - (CONV_HALO - (width - 1))
                    if 0 <= k < width:
                        term = shifted[SUBLANES * a:SUBLANES * a + CONV_STRIP, :] * convw_ref[k:k + 1, :]
                        acc = term if acc is None else acc + term
            strips.append(acc)
            yield MICRO
        acc = jnp.concatenate(strips, axis=0)
        yield
        pb = yield from _tiled_dot(hn, win_ref, 2 * dc, 2 * dc + 2 * ds)
        ya = _layer_norm(acc + convb_ref[...], clng_ref[...], clnb_ref[...])
        ya = (ya * jax.nn.sigmoid(ya)).astype(BF16)
        yield
        zb = 0.5 * pb * (1.0 + lax.erf(pb * (2.0 ** -0.5)))
        bu = zb[:, :ds]
        bv = _layer_norm(zb[:, ds:], slng_ref[...], slnb_ref[...]).astype(BF16)
        yb = []
        for hh in range(SGU_HEADS):
            cs = slice(hh * hd, (hh + 1) * hd)
            mixed = [_dot(w_tril[hh], bv[c0:c0 + SGU_CHUNK, cs]) + sgub_ref[hh] for c0 in range(0, sub, SGU_CHUNK)]
            yb.append((bu[:, cs] * jnp.concatenate(mixed, axis=0)).astype(BF16))
            yield MICRO
        yield
        m = yield from _tiled_dot(jnp.concatenate([ya] + yb, axis=-1), wout_ref, 0, d)
        o_ref[r0:r0 + sub, :] = x_ref[r0:r0 + sub, :] + _rms(m, post_g_ref[...])

    _run_skewed([chain(r0) for r0 in range(0, tm, sub)], EVEN_SKEW)
    zbuf_ref[0:CONV_HALO, :] = zbuf_ref[tm:tm + CONV_HALO, :]


def _even(h, pre_g, win, convw, convb, clng, clnb, slng, slnb, sguw, sgub, wout, post_g, seq):
    n, d = h.shape
    width, dc = convw.shape
    ds = d - dc
    hd = ds // SGU_HEADS
    assert width - 1 <= CONV_HALO and EVEN_SUB_ROWS % SGU_CHUNK == 0
    return pl.pallas_call(
        functools.partial(_even_body, nb=seq // ROW_TILE),
        out_shape=jax.ShapeDtypeStruct((n, d), F32),
        grid=(n // ROW_TILE,),
        in_specs=[_rows(ROW_TILE, d), _resident((1, d)), _resident((d, 2 * dc + 2 * ds)),
                  _resident((width, dc)), _resident((1, dc)), _resident((1, dc)), _resident((1, dc)),
                  _resident((1, ds)), _resident((1, ds)),
                  _resident((SGU_HEADS, SGU_CHUNK, SGU_CHUNK)), _resident((SGU_HEADS, SGU_CHUNK, hd)),
                  _resident((d, d)), _resident((1, d))],
        out_specs=_rows(ROW_TILE, d),
        scratch_shapes=[pltpu.VMEM((CONV_HALO + ROW_TILE, dc), F32)],
        compiler_params=_params(),
        name="conv_sgu_mixer",
    )(h, pre_g, win, convw, convb, clng, clnb, slng, slnb, sguw, sgub, wout, post_g)


def kernel(x, mem, ffn1_pre_g, ffn1_w_gu, ffn1_w_down, ffn1_post_g, mix_pre_g, mix_post_g, ev_w_in, ev_conv_w, ev_conv_b, ev_conv_ln_g, ev_conv_ln_b, ev_sgu_ln_g, ev_sgu_ln_b, ev_sgu_w, ev_sgu_b, ev_w_out, od_w_in, od_w_group, od_scale, od_w_out, xa_pre_g, xa_mem_g, xa_w_q, xa_w_kv, xa_w_o, xa_post_g, ffn2_pre_g, ffn2_w_gu, ffn2_w_down, ffn2_post_g):
    bsz, seq, d = x.shape
    depth = ffn1_w_gu.shape[0]
    hd = (d - ev_conv_w.shape[2]) // SGU_HEADS
    tiles = ((ROW_TILE, FFN_SUB_ROWS), (ROW_TILE, EVEN_SUB_ROWS), (XA_ROW_TILE, XA_SUB_ROWS), (ODD_ROW_TILE, ODD_SUB_ROWS))
    assert all(seq % tile == 0 and tile % sub == 0 for tile, sub in tiles)

    def row(v):
        return v.reshape(1, -1)

    h = x.reshape(bsz * seq, d)
    od_w_group2 = od_w_group.reshape(od_w_group.shape[0], -1, od_w_group.shape[-1])
    wgu, wd = ffn1_w_gu[0].astype(BF16), ffn1_w_down[0].astype(BF16)
    for i in range(depth):
        m = i // 2
        mix = [(ev_w_in, m), (ev_w_out, m)] if i % 2 == 0 else [(od_w_in, m), (od_w_group2, m), (od_w_out, m)]
        h, w = _ffn(h, row(ffn1_pre_g[i]), wgu, wd, row(ffn1_post_g[i]),
                    mix + [(xa_w_q, i), (xa_w_kv, i), (xa_w_o, i), (ffn2_w_gu, i), (ffn2_w_down, i)])
        *w_mix, wq, wkv, wo, wgu, wd = w
        if i % 2 == 0:
            sgub = jnp.broadcast_to(ev_sgu_b[m][:, :, None], (SGU_HEADS, SGU_CHUNK, hd))
            h = _even(h, row(mix_pre_g[i]), w_mix[0], ev_conv_w[m], row(ev_conv_b[m]),
                      row(ev_conv_ln_g[m]), row(ev_conv_ln_b[m]), row(ev_sgu_ln_g[m]), row(ev_sgu_ln_b[m]),
                      ev_sgu_w[m], sgub, w_mix[1], row(mix_post_g[i]), seq)
        else:
            h = _odd(h, row(mix_pre_g[i]), w_mix[0], w_mix[1].reshape(od_w_group.shape[1:]), row(od_scale[m]),
                     w_mix[2], row(mix_post_g[i]), seq)
        k, v = _kv(mem, row(xa_mem_g[i]), wkv)
        h = _xa(h, row(xa_pre_g[i]), wq, k, v, wo, row(xa_post_g[i]), seq)
        nxt = [(ffn1_w_gu, i + 1), (ffn1_w_down, i + 1)] if i + 1 < depth else []
        h, w = _ffn(h, row(ffn2_pre_g[i]), wgu, wd, row(ffn2_post_g[i]), nxt)
        if nxt:
            wgu, wd = w
    return h.reshape(bsz, seq, d)
```

```python
import functools

import jax
import jax.numpy as jnp
from jax import lax
from jax.experimental import pallas as pl
from jax.experimental.pallas import tpu as pltpu

F32 = jnp.float32
BF16 = jnp.bfloat16

EPS = 1e-6
SGU_HEADS = 4
SGU_CHUNK = 128
POOL_WINDOWS = (2, 4, 8, 16)
XA_HEADS = 4

SUBLANES = 8
BF16_SUBLANES = 16
ROW_TILE = 1024
XA_ROW_TILE = 2048
ODD_ROW_TILE = 2048
FFN_SUB_ROWS, FFN_SKEW = 256, 1
XA_SUB_ROWS, XA_SKEW = 512, None
ODD_SUB_ROWS, ODD_SKEW = 256, 1
EVEN_SUB_ROWS, EVEN_SKEW = 512, 1
FF_CHUNK = 1024
MXU_COLS = 256
CONV_STRIP = 128
CONV_HALO = 32
POOL_HALO = 16
VMEM_LIMIT = 56 * 1024 * 1024


def _rms(x, g):
    return x * lax.rsqrt(jnp.mean(x * x, axis=-1, keepdims=True) + EPS) * g


def _layer_norm(x, g, b):
    mu = jnp.mean(x, axis=-1, keepdims=True)
    xc = x - mu
    var = jnp.mean(xc * xc, axis=-1, keepdims=True)
    return xc * lax.rsqrt(var + EPS) * g + b


def _dot(a, b):
    return jnp.dot(a, b, preferred_element_type=F32)


def _resident(shape):
    zeros = (0,) * len(shape)
    return pl.BlockSpec(shape, lambda i: zeros, pipeline_mode=pl.Buffered(1))


def _rows(tm, d):
    return pl.BlockSpec((tm, d), lambda i: (i, 0))


def _params():
    return pltpu.CompilerParams(dimension_semantics=("arbitrary",), vmem_limit_bytes=VMEM_LIMIT)


MICRO = "micro"


def _run_skewed(chains, skew):
    if skew is None:
        for chain in chains:
            for _ in chain:
                pass
        return
    done = [False] * len(chains)
    tick = 0
    while not all(done):
        active = [c for c in range(len(chains)) if not done[c] and tick >= c * skew]
        while active:
            for c in list(active):
                try:
                    if next(chains[c]) is not MICRO:
                        active.remove(c)
                except StopIteration:
                    done[c] = True
                    active.remove(c)
        tick += 1


def _tiled_dot(lhs, w_ref, c0, c1):
    tiles = []
    for t0 in range(c0, c1, MXU_COLS):
        tiles.append(_dot(lhs, w_ref[:, t0:min(t0 + MXU_COLS, c1)]))
        yield MICRO
    return jnp.concatenate(tiles, axis=-1)


def _cast_plan(stacked, layer, steps):
    _, r, c = stacked.shape
    br = next(b for b in range(BF16_SUBLANES, r + 1, BF16_SUBLANES) if r % b == 0 and r // b <= steps)
    last = r // br - 1
    in_spec = pl.BlockSpec((None, br, c), lambda i: (layer, jnp.minimum(i, last), 0))
    out_spec = pl.BlockSpec((br, c), lambda i: (jnp.minimum(i, last), 0))
    return in_spec, out_spec, jax.ShapeDtypeStruct((r, c), BF16)


def _ffn_body(x_ref, pre_g_ref, wgu_ref, wd_ref, post_g_ref, *rest, chunks, d_ff, n_cast):
    cast_in, o_ref, cast_out = rest[:n_cast], rest[n_cast], rest[n_cast + 1:]

    def chain(rs):
        hn = _rms(x_ref[rs, :], pre_g_ref[...]).astype(BF16)
        yield
        f = None
        for c0, c1 in chunks:
            g = _dot(hn, wgu_ref[:, c0:c1])
            u = _dot(hn, wgu_ref[:, d_ff + c0:d_ff + c1])
            a = (g * jax.nn.sigmoid(g) * u).astype(BF16)
            fc = _dot(a, wd_ref[c0:c1, :])
            f = fc if f is None else f + fc
            yield
        o_ref[rs, :] = x_ref[rs, :] + 0.5 * _rms(f, post_g_ref[...])

    def casts():
        for src, dst in zip(cast_in, cast_out):
            dst[...] = src[...].astype(BF16)
            yield MICRO

    rows = [chain(slice(r0, r0 + FFN_SUB_ROWS)) for r0 in range(0, x_ref.shape[0], FFN_SUB_ROWS)]
    _run_skewed(rows + [casts()], FFN_SKEW)


def _ffn(h, pre_g, wgu, wd, post_g, casts=()):
    n, d = h.shape
    d_ff = wd.shape[0]
    steps = n // ROW_TILE
    chunks = tuple((c, min(c + FF_CHUNK, d_ff)) for c in range(0, d_ff, FF_CHUNK))
    plans = [_cast_plan(stack, layer, steps) for stack, layer in casts]
    outs = pl.pallas_call(
        functools.partial(_ffn_body, chunks=chunks, d_ff=d_ff, n_cast=len(plans)),
        out_shape=[jax.ShapeDtypeStruct((n, d), F32)] + [p[2] for p in plans],
        grid=(steps,),
        in_specs=[_rows(ROW_TILE, d), _resident((1, d)), _resident((d, 2 * d_ff)), _resident((d_ff, d)),
                  _resident((1, d))] + [p[0] for p in plans],
        out_specs=[_rows(ROW_TILE, d)] + [p[1] for p in plans],
        compiler_params=_params(),
        name="swiglu_ffn",
    )(h, pre_g, wgu, wd, post_g, *[stack for stack, _ in casts])
    return outs[0], outs[1:]


def _xa_body(x_ref, mem_ref, pre_g_ref, mem_g_ref, wq_ref, wkv_ref, wo_ref, post_g_ref, o_ref, k_ref, v_ref, *, heads, nb):
    d = x_ref.shape[1]
    hd = d // heads

    @pl.when(pl.program_id(0) % nb == 0)
    def _():
        memn = _rms(mem_ref[0], mem_g_ref[...]).astype(BF16)
        k_ref[...] = _dot(memn, wkv_ref[:, :d]).astype(BF16)
        v_ref[...] = _dot(memn, wkv_ref[:, d:]).astype(BF16)

    def chain(rs):
        hn = _rms(x_ref[rs, :], pre_g_ref[...]).astype(BF16)
        q = (_dot(hn, wq_ref[...]) * (hd ** -0.5)).astype(BF16)
        yield
        outs = []
        for h in range(heads):
            sl = slice(h * hd, (h + 1) * hd)
            s = lax.dot_general(q[:, sl], k_ref[:, sl], (((1,), (1,)), ((), ())),
                                preferred_element_type=F32)
            p = jnp.exp(s - jnp.max(s, axis=-1, keepdims=True))
            l = jnp.sum(p, axis=-1, keepdims=True)
            outs.append((_dot(p.astype(BF16), v_ref[:, sl]) / l).astype(BF16))
            yield
        c = _dot(jnp.concatenate(outs, axis=-1), wo_ref[...])
        o_ref[rs, :] = x_ref[rs, :] + _rms(c, post_g_ref[...])

    _run_skewed([chain(slice(r0, r0 + XA_SUB_ROWS)) for r0 in range(0, x_ref.shape[0], XA_SUB_ROWS)], XA_SKEW)


def _xa(h, mem, pre_g, mem_g, wq, wkv, wo, post_g, seq):
    n, d = h.shape
    m = mem.shape[1]
    nb = seq // XA_ROW_TILE
    return pl.pallas_call(
        functools.partial(_xa_body, heads=XA_HEADS, nb=nb),
        out_shape=jax.ShapeDtypeStruct((n, d), F32),
        grid=(n // XA_ROW_TILE,),
        in_specs=[_rows(XA_ROW_TILE, d), pl.BlockSpec((1, m, d), lambda i: (i // nb, 0, 0)),
                  _resident((1, d)), _resident((1, d)), _resident((d, d)), _resident((d, 2 * d)),
                  _resident((d, d)), _resident((1, d))],
        out_specs=_rows(XA_ROW_TILE, d),
        scratch_shapes=[pltpu.VMEM((m, d), BF16), pltpu.VMEM((m, d), BF16)],
        compiler_params=_params(),
        name="xattn",
    )(h, mem, pre_g, mem_g, wq, wkv, wo, post_g)


def _odd_body(x_ref, pre_g_ref, win_ref, wgrp_ref, scale_ref, wout_ref, post_g_ref, o_ref, pbuf_ref, *, nb):
    tm, d = x_ref.shape
    gd = d // len(POOL_WINDOWS)
    sub = ODD_SUB_ROWS
    j = pl.program_id(0) % nb

    @pl.when(j == 0)
    def _():
        pbuf_ref[0:POOL_HALO, :] = jnp.zeros((POOL_HALO, d), F32)

    def chain(r0):
        hn = _rms(x_ref[r0:r0 + sub, :], pre_g_ref[...]).astype(BF16)
        p = yield from _tiled_dot(hn, win_ref, 0, d)
        pbuf_ref[POOL_HALO + r0:POOL_HALO + r0 + sub, :] = p
        yield
        s = pbuf_ref[r0:POOL_HALO + r0 + sub, :]
        pos = j * tm + r0 + 1 + lax.broadcasted_iota(jnp.int32, (sub, 1), 0)
        pieces = []
        width = 1
        for w in POOL_WINDOWS:
            while width < w:
                s = s + pltpu.roll(s, width, 0)
                width *= 2
            count = jnp.minimum(pos, w).astype(F32)
            pieces.append((s[POOL_HALO:, :gd] / count - p[:, :gd]).astype(BF16))
            if w != POOL_WINDOWS[-1]:
                s = s[:, gd:]
                p = p[:, gd:]
            yield MICRO
        yield
        e = []
        for gi in range(len(POOL_WINDOWS)):
            e.append(_dot(pieces[gi], wgrp_ref[gi]))
            yield MICRO
        e = (jnp.concatenate(e, axis=-1) * scale_ref[...]).astype(BF16)
        m = yield from _tiled_dot(e, wout_ref, 0, d)
        o_ref[r0:r0 + sub, :] = x_ref[r0:r0 + sub, :] + _rms(m, post_g_ref[...])

    _run_skewed([chain(r0) for r0 in range(0, tm, sub)], ODD_SKEW)
    pbuf_ref[0:POOL_HALO, :] = pbuf_ref[tm:tm + POOL_HALO, :]


def _odd(h, pre_g, win, wgrp, scale, wout, post_g, seq):
    n, d = h.shape
    g, gd, _ = wgrp.shape
    return pl.pallas_call(
        functools.partial(_odd_body, nb=seq // ODD_ROW_TILE),
        out_shape=jax.ShapeDtypeStruct((n, d), F32),
        grid=(n // ODD_ROW_TILE,),
        in_specs=[_rows(ODD_ROW_TILE, d), _resident((1, d)), _resident((d, d)), _resident((g, gd, gd)),
                  _resident((1, d)), _resident((d, d)), _resident((1, d))],
        out_specs=_rows(ODD_ROW_TILE, d),
        scratch_shapes=[pltpu.VMEM((POOL_HALO + ODD_ROW_TILE, d), F32)],
        compiler_params=_params(),
        name="pool_mixer",
    )(h, pre_g, win, wgrp, scale, wout, post_g)


def _even_body(x_ref, pre_g_ref, win_ref, convw_ref, convb_ref, clng_ref, clnb_ref, slng_ref, slnb_ref,
               sguw_ref, sgub_ref, wout_ref, post_g_ref, o_ref, zbuf_ref, *, nb):
    tm, d = x_ref.shape
    dc = convw_ref.shape[1]
    ds = d - dc
    width = convw_ref.shape[0]
    hd = ds // SGU_HEADS
    sub = EVEN_SUB_ROWS
    j = pl.program_id(0) % nb

    @pl.when(j == 0)
    def _():
        zbuf_ref[0:CONV_HALO, :] = jnp.zeros((CONV_HALO, dc), F32)

    row = lax.broadcasted_iota(jnp.int32, (SGU_CHUNK, SGU_CHUNK), 0)
    col = lax.broadcasted_iota(jnp.int32, (SGU_CHUNK, SGU_CHUNK), 1)
    w_tril = [jnp.where(col <= row, sguw_ref[hh], 0.0).astype(BF16) for hh in range(SGU_HEADS)]

    def chain(r0):
        hn = _rms(x_ref[r0:r0 + sub, :], pre_g_ref[...]).astype(BF16)
        pa = yield from _tiled_dot(hn, win_ref, 0, 2 * dc)
        zbuf_ref[CONV_HALO + r0:CONV_HALO + r0 + sub, :] = pa[:, :dc] * jax.nn.sigmoid(pa[:, dc:])
        yield
        strips = []
        for t0 in range(0, sub, CONV_STRIP):
            ext = zbuf_ref[r0 + t0:r0 + t0 + CONV_STRIP + CONV_HALO, :]
            n_ext = CONV_STRIP + CONV_HALO
            acc = None
            for b in range(SUBLANES):
                shifted = ext if b == 0 else pltpu.roll(ext, n_ext - b, 0)
                for a in range(CONV_HALO // SUBLANES + 1):
                    k = SUBLANES * a + b - (CONV_HALO - (width - 1))
                    if 0 <= k < width:
                        term = shifted[SUBLANES * a:SUBLANES * a + CONV_STRIP, :] * convw_ref[k:k + 1, :]
                        acc = term if acc is None else acc + term
            strips.append(acc)
            yield MICRO
        acc = jnp.concatenate(strips, axis=0)
        yield
        pb = yield from _tiled_dot(hn, win_ref, 2 * dc, 2 * dc + 2 * ds)
        ya = _layer_norm(acc + convb_ref[...], clng_ref[...], clnb_ref[...])
        ya = (ya * jax.nn.sigmoid(ya)).astype(BF16)
        yield
        zb = 0.5 * pb * (1.0 + lax.erf(pb * (2.0 ** -0.5)))
        bu = zb[:, :ds]
        bv = _layer_norm(zb[:, ds:], slng_ref[...], slnb_ref[...]).astype(BF16)
        yb = []
        for hh in range(SGU_HEADS):
            cs = slice(hh * hd, (hh + 1) * hd)
            mixed = [_dot(w_tril[hh], bv[c0:c0 + SGU_CHUNK, cs]) + sgub_ref[hh] for c0 in range(0, sub, SGU_CHUNK)]
            yb.append((bu[:, cs] * jnp.concatenate(mixed, axis=0)).astype(BF16))
            yield MICRO
        yield
        m = yield from _tiled_dot(jnp.concatenate([ya] + yb, axis=-1), wout_ref, 0, d)
        o_ref[r0:r0 + sub, :] = x_ref[r0:r0 + sub, :] + _rms(m, post_g_ref[...])

    _run_skewed([chain(r0) for r0 in range(0, tm, sub)], EVEN_SKEW)
    zbuf_ref[0:CONV_HALO, :] = zbuf_ref[tm:tm + CONV_HALO, :]


def _even(h, pre_g, win, convw, convb, clng, clnb, slng, slnb, sguw, sgub, wout, post_g, seq):
    n, d = h.shape
    width, dc = convw.shape
    ds = d - dc
    hd = ds // SGU_HEADS
    assert width - 1 <= CONV_HALO and EVEN_SUB_ROWS % SGU_CHUNK == 0
    return pl.pallas_call(
        functools.partial(_even_body, nb=seq // ROW_TILE),
        out_shape=jax.ShapeDtypeStruct((n, d), F32),
        grid=(n // ROW_TILE,),
        in_specs=[_rows(ROW_TILE, d), _resident((1, d)), _resident((d, 2 * dc + 2 * ds)),
                  _resident((width, dc)), _resident((1, dc)), _resident((1, dc)), _resident((1, dc)),
                  _resident((1, ds)), _resident((1, ds)),
                  _resident((SGU_HEADS, SGU_CHUNK, SGU_CHUNK)), _resident((SGU_HEADS, SGU_CHUNK, hd)),
                  _resident((d, d)), _resident((1, d))],
        out_specs=_rows(ROW_TILE, d),
        scratch_shapes=[pltpu.VMEM((CONV_HALO + ROW_TILE, dc), F32)],
        compiler_params=_params(),
        name="conv_sgu_mixer",
    )(h, pre_g, win, convw, convb, clng, clnb, slng, slnb, sguw, sgub, wout, post_g)


def kernel(x, mem, ffn1_pre_g, ffn1_w_gu, ffn1_w_down, ffn1_post_g, mix_pre_g, mix_post_g, ev_w_in, ev_conv_w, ev_conv_b, ev_conv_ln_g, ev_conv_ln_b, ev_sgu_ln_g, ev_sgu_ln_b, ev_sgu_w, ev_sgu_b, ev_w_out, od_w_in, od_w_group, od_scale, od_w_out, xa_pre_g, xa_mem_g, xa_w_q, xa_w_kv, xa_w_o, xa_post_g, ffn2_pre_g, ffn2_w_gu, ffn2_w_down, ffn2_post_g):
    bsz, seq, d = x.shape
    depth = ffn1_w_gu.shape[0]
    hd = (d - ev_conv_w.shape[2]) // SGU_HEADS
    tiles = ((ROW_TILE, FFN_SUB_ROWS), (ROW_TILE, EVEN_SUB_ROWS), (XA_ROW_TILE, XA_SUB_ROWS), (ODD_ROW_TILE, ODD_SUB_ROWS))
    assert all(seq % tile == 0 and tile % sub == 0 for tile, sub in tiles)

    def row(v):
        return v.reshape(1, -1)

    h = x.reshape(bsz * seq, d)
    od_w_group2 = od_w_group.reshape(od_w_group.shape[0], -1, od_w_group.shape[-1])
    wgu, wd = ffn1_w_gu[0].astype(BF16), ffn1_w_down[0].astype(BF16)
    for i in range(depth):
        m = i // 2
        mix = [(ev_w_in, m), (ev_w_out, m)] if i % 2 == 0 else [(od_w_in, m), (od_w_group2, m), (od_w_out, m)]
        h, w = _ffn(h, row(ffn1_pre_g[i]), wgu, wd, row(ffn1_post_g[i]),
                    mix + [(xa_w_q, i), (xa_w_kv, i), (xa_w_o, i), (ffn2_w_gu, i), (ffn2_w_down, i)])
        *w_mix, wq, wkv, wo, wgu, wd = w
        if i % 2 == 0:
            sgub = jnp.broadcast_to(ev_sgu_b[m][:, :, None], (SGU_HEADS, SGU_CHUNK, hd))
            h = _even(h, row(mix_pre_g[i]), w_mix[0], ev_conv_w[m], row(ev_conv_b[m]),
                      row(ev_conv_ln_g[m]), row(ev_conv_ln_b[m]), row(ev_sgu_ln_g[m]), row(ev_sgu_ln_b[m]),
                      ev_sgu_w[m], sgub, w_mix[1], row(mix_post_g[i]), seq)
        else:
            h = _odd(h, row(mix_pre_g[i]), w_mix[0], w_mix[1].reshape(od_w_group.shape[1:]), row(od_scale[m]),
                     w_mix[2], row(mix_post_g[i]), seq)
        h = _xa(h, mem, row(xa_pre_g[i]), row(xa_mem_g[i]), wq, wkv, wo, row(xa_post_g[i]), seq)
        nxt = [(ffn1_w_gu, i + 1), (ffn1_w_down, i + 1)] if i + 1 < depth else []
        h, w = _ffn(h, row(ffn2_pre_g[i]), wgu, wd, row(ffn2_post_g[i]), nxt)
        if nxt:
            wgu, wd = w
    return h.reshape(bsz, seq, d)
```
